```python
import jax, jax.numpy as jnp
from jax import lax
import numpy as np

D_MODEL = 4096
BATCH = 2
SEQ = 4096
DEPTH = 2

GRID_W = 64
CTX_LEN = 256
N_MIXERS = 2
EPS = 1e-6
FNET_GROUPS = 8
FNET_GROUP_DIM = D_MODEL // FNET_GROUPS
HGRN_EXPAND = 128
HGRN_HEADS = D_MODEL // HGRN_EXPAND
HGRN_DK = HGRN_EXPAND
HGRN_DV = D_MODEL // HGRN_HEADS
FORGET_DIM = HGRN_HEADS * HGRN_DK
CHUNK = 64
N_EXPERTS = 64
TOP_K = 8
N_GROUPS = 8
TOPK_GROUPS = 4
EXPERT_DIM = D_MODEL // 16
SHARED_DIM = D_MODEL // 4
ROUTED_SCALE = 2.5
N_A = (DEPTH + 1) // 2
N_B = DEPTH // 2

kernel_name = "hybrid_fnet_hgrn2_moe_dit"


def rmsnorm(x, g):
    xf = x.astype(jnp.float32)
    y = xf * lax.rsqrt(jnp.mean(xf * xf, axis=-1, keepdims=True) + EPS)
    return (y * g.astype(jnp.float32)).astype(x.dtype)


def ada_mod(cvec, w, b):
    m = jnp.matmul(jax.nn.silu(cvec), w) + b
    return jnp.split(m, 6, axis=-1)


def modulate(h, shift, scale):
    return h * (1 + scale) + shift


def fourier_mix(h, w_in, w_out):
    bsz, t_len, _ = h.shape
    u = jnp.matmul(h, w_in).reshape(bsz, t_len, FNET_GROUPS, FNET_GROUP_DIM).astype(jnp.float32)
    z = jnp.fft.fftn(u, axes=(1, 3), norm="ortho").real
    return jnp.matmul(z.reshape(bsz, t_len, D_MODEL).astype(h.dtype), w_out)


def gla_scan(q, k, v, logf, s0):
    bsz, t_len, heads, _ = q.shape
    n_chunks = t_len // CHUNK

    def to_chunks(a):
        return jnp.moveaxis(a.reshape(bsz, n_chunks, CHUNK, heads, a.shape[-1]), 1, 0)

    tri = jnp.tril(jnp.ones((CHUNK, CHUNK), dtype=bool))[None, :, :, None, None]

    def step(state, inp):
        qi, ki, vi, gi = inp
        b = jnp.cumsum(gi, axis=1)
        o_inter = jnp.einsum('bchk,bhkv->bchv', qi * jnp.exp(b), state)
        diff = b[:, :, None] - b[:, None, :]
        decay = jnp.exp(jnp.where(tri, diff, -jnp.inf))
        scores = jnp.einsum('bthk,bshk,btshk->bhts', qi, ki, decay)
        o_intra = jnp.einsum('bhts,bshv->bthv', scores, vi)
        b_last = b[:, -1]
        k_dec = ki * jnp.exp(b_last[:, None] - b)
        new_state = jnp.exp(b_last)[..., None] * state + jnp.einsum('bshk,bshv->bhkv', k_dec, vi)
        return new_state, o_inter + o_intra

    s_fin, o = lax.scan(step, s0, (to_chunks(q), to_chunks(k), to_chunks(v), to_chunks(logf)))
    o = jnp.moveaxis(o, 0, 1).reshape(bsz, t_len, heads, v.shape[-1])
    return o, s_fin


def hgrn_project(h, w_in, lb):
    bsz, t_len, _ = h.shape
    p = jnp.matmul(h, w_in)
    q, f_fw, f_bw, v, g = jnp.split(p, [FORGET_DIM, 2 * FORGET_DIM, 3 * FORGET_DIM, 3 * FORGET_DIM + D_MODEL], axis=-1)

    def heads(a):
        return a.reshape(bsz, t_len, HGRN_HEADS, -1).astype(jnp.float32)

    q = heads(q) * HGRN_DK ** -0.5
    v = heads(v)
    ks, logfs = [], []
    for d, f_raw in enumerate((f_fw, f_bw)):
        fr = heads(f_raw)
        lbd = lb[d].reshape(HGRN_HEADS, HGRN_DK)
        logfs.append(jnp.logaddexp(jnp.log(lbd), jnp.log1p(-lbd) + jax.nn.log_sigmoid(fr)))
        ks.append((1 - lbd) * jax.nn.sigmoid(-fr))
    return q, v, g, ks, logfs


def hgrn_readout(o, g, gn_g, w_out):
    bsz, t_len = o.shape[:2]
    o = o * lax.rsqrt(jnp.mean(o * o, axis=-1, keepdims=True) + EPS) * gn_g.astype(jnp.float32)
    o = o.reshape(bsz, t_len, D_MODEL) * jax.nn.silu(g.astype(jnp.float32))
    return jnp.matmul(o.astype(g.dtype), w_out)


def _rev(a, d):
    return jnp.flip(a, axis=1) if d else a


def hgrn_mix(a_lat, a_ctx, w_in, lb, gn_g, w_out, want_ctx):
    qc, vc, gc, kcs, fcs = hgrn_project(a_ctx, w_in, lb)
    ql, vl, gl, kls, fls = hgrn_project(a_lat, w_in, lb)
    s0 = jnp.zeros((a_lat.shape[0], HGRN_HEADS, HGRN_DK, HGRN_DV), jnp.float32)
    o_lat = 0.0
    o_ctx = 0.0
    for d in range(2):
        oc, s_ctx = gla_scan(_rev(qc, d), _rev(kcs[d], d), _rev(vc, d), _rev(fcs[d], d), s0)
        ol, _ = gla_scan(_rev(ql, d), _rev(kls[d], d), _rev(vl, d), _rev(fls[d], d), s_ctx)
        o_lat = o_lat + _rev(ol, d)
        o_ctx = o_ctx + _rev(oc, d)
    y_lat = hgrn_readout(o_lat, gl, gn_g, w_out)
    y_ctx = hgrn_readout(o_ctx, gc, gn_g, w_out) if want_ctx else None
    return y_lat, y_ctx


def moe(h, router_w, router_b, w_gate, w_up, w_down, sw_gate, sw_up, sw_down):
    shp = h.shape
    t = h.reshape(-1, D_MODEL)
    n_tok = t.shape[0]
    scores = jax.nn.sigmoid(jnp.matmul(t, router_w).astype(jnp.float32))
    choice = scores + router_b.astype(jnp.float32)
    grp = choice.reshape(n_tok, N_GROUPS, N_EXPERTS // N_GROUPS)
    grp_score = jnp.sum(lax.top_k(grp, 2)[0], axis=-1)
    _, gidx = lax.top_k(grp_score, TOPK_GROUPS)
    gmask = jnp.sum(jax.nn.one_hot(gidx, N_GROUPS, dtype=jnp.float32), axis=1)
    emask = jnp.repeat(gmask, N_EXPERTS // N_GROUPS, axis=1)
    _, eidx = lax.top_k(jnp.where(emask > 0, choice, -jnp.inf), TOP_K)
    w = jnp.take_along_axis(scores, eidx, axis=1)
    w = w / jnp.sum(w, axis=-1, keepdims=True) * ROUTED_SCALE
    combine = jnp.zeros((n_tok, N_EXPERTS), jnp.float32).at[jnp.arange(n_tok)[:, None], eidx].set(w)
    hg = jnp.einsum('nd,edf->nef', t, w_gate)
    hu = jnp.einsum('nd,edf->nef', t, w_up)
    act = jax.nn.silu(hg) * hu * combine[..., None].astype(t.dtype)
    y = jnp.einsum('nef,efd->nd', act, w_down)
    y = y + jnp.matmul(jax.nn.silu(jnp.matmul(t, sw_gate)) * jnp.matmul(t, sw_up), sw_down)
    return y.reshape(shp)


def setup_inputs(seed: int = 0) -> dict:
    key = jax.random.key(seed)
    ks = jax.random.split(key, 24)
    nrm = jax.random.normal
    f32 = jnp.float32
    D = D_MODEL
    return {
        "x": nrm(ks[0], (BATCH, SEQ, D), f32),
        "c": nrm(ks[1], (BATCH, D), f32),
        "ctx": nrm(ks[2], (BATCH, CTX_LEN, D), f32),
        "c_ctx": nrm(ks[3], (D,), f32),
        "ada_w": nrm(ks[4], (DEPTH, D, 6 * D), f32) * (0.5 * D ** -0.5),
        "ada_b": nrm(ks[5], (DEPTH, 6 * D), f32) * 0.01,
        "norm1_g": 1.0 + 0.1 * nrm(ks[6], (DEPTH, D), f32),
        "norm2_g": 1.0 + 0.1 * nrm(ks[7], (DEPTH, D), f32),
        "fnet_w_in": nrm(ks[8], (N_A, D, D), f32) * D ** -0.5,
        "fnet_w_out": nrm(ks[9], (N_A, D, D), f32) * D ** -0.5,
        "hgrn_w_in": nrm(ks[10], (N_B, D, 3 * FORGET_DIM + 2 * D), f32) * D ** -0.5,
        "hgrn_lb": 0.5 * nrm(ks[11], (DEPTH, 2, FORGET_DIM), f32),
        "hgrn_gnorm_g": 1.0 + 0.1 * nrm(ks[12], (N_B, HGRN_DV), f32),
        "hgrn_w_out": nrm(ks[13], (N_B, D, D), f32) * D ** -0.5,
        "router_w": nrm(ks[14], (DEPTH, D, N_EXPERTS), f32) * D ** -0.5,
        "router_b": nrm(ks[15], (DEPTH, N_EXPERTS), f32) * 0.01,
        "exp_w_gate": nrm(ks[16], (DEPTH, N_EXPERTS, D, EXPERT_DIM), f32) * D ** -0.5,
        "exp_w_up": nrm(ks[17], (DEPTH, N_EXPERTS, D, EXPERT_DIM), f32) * D ** -0.5,
        "exp_w_down": nrm(ks[18], (DEPTH, N_EXPERTS, EXPERT_DIM, D), f32) * EXPERT_DIM ** -0.5,
        "sh_w_gate": nrm(ks[19], (DEPTH, D, SHARED_DIM), f32) * D ** -0.5,
        "sh_w_up": nrm(ks[20], (DEPTH, D, SHARED_DIM), f32) * D ** -0.5,
        "sh_w_down": nrm(ks[21], (DEPTH, SHARED_DIM, D), f32) * SHARED_DIM ** -0.5,
        "final_g": 1.0 + 0.1 * nrm(ks[22], (D,), f32),
    }


def reference(x, c, ctx, c_ctx, ada_w, ada_b, norm1_g, norm2_g, fnet_w_in, fnet_w_out,
              hgrn_w_in, hgrn_lb, hgrn_gnorm_g, hgrn_w_out, router_w, router_b,
              exp_w_gate, exp_w_up, exp_w_down, sh_w_gate, sh_w_up, sh_w_down, final_g):
    lb_all = jnp.cumsum(jax.nn.softmax(hgrn_lb.astype(jnp.float32), axis=0), axis=0)
    lb_all = lb_all - lb_all[0]
    x_lat, x_ctx = x, ctx
    for i in range(DEPTH):
        last = i == DEPTH - 1
        j = i // N_MIXERS
        sh1, sc1, gt1, sh2, sc2, gt2 = ada_mod(c[:, None, :], ada_w[i], ada_b[i])
        ch1, cs1, cg1, ch2, cs2, cg2 = ada_mod(c_ctx[None, None, :], ada_w[i], ada_b[i])
        a_lat = modulate(rmsnorm(x_lat, norm1_g[i]), sh1, sc1)
        a_ctx = modulate(rmsnorm(x_ctx, norm1_g[i]), ch1, cs1)
        if i % N_MIXERS == 0:
            y_lat = fourier_mix(a_lat, fnet_w_in[j], fnet_w_out[j])
            y_ctx = None if last else fourier_mix(a_ctx, fnet_w_in[j], fnet_w_out[j])
        else:
            y_lat, y_ctx = hgrn_mix(a_lat, a_ctx, hgrn_w_in[j], lb_all[i], hgrn_gnorm_g[j],
                                    hgrn_w_out[j], not last)
        x_lat = x_lat + gt1 * y_lat
        m_lat = modulate(rmsnorm(x_lat, norm2_g[i]), sh2, sc2)
        x_lat = x_lat + gt2 * moe(m_lat, router_w[i], router_b[i], exp_w_gate[i], exp_w_up[i],
                                  exp_w_down[i], sh_w_gate[i], sh_w_up[i], sh_w_down[i])
        if not last:
            x_ctx = x_ctx + cg1 * y_ctx
            m_ctx = modulate(rmsnorm(x_ctx, norm2_g[i]), ch2, cs2)
            x_ctx = x_ctx + cg2 * moe(m_ctx, router_w[i], router_b[i], exp_w_gate[i], exp_w_up[i],
                                      exp_w_down[i], sh_w_gate[i], sh_w_up[i], sh_w_down[i])
    return rmsnorm(x_lat, final_g)
```

```python
import functools
import math

import jax
import jax.numpy as jnp
import numpy as np
from jax import lax
from jax.experimental import pallas as pl
from jax.experimental.pallas import tpu as pltpu

EPS = 1e-6
FNET_GROUPS = 8
HEAD_DIM = 128
ROUTE_GROUPS = 8
ROUTE_TOPK_GROUPS = 4
ROUTE_TOP_K = 8
ROUTED_SCALE = 2.5
GLA_CHUNK = 64
GLA_BLOCK = 16
GLA_STEP_ROWS = 256
GLA_HEADS_PER_STEP = 2
MOD_ROWS = 8

V7X_VMEM_LIMIT_BYTES = 56 * 1024 * 1024

f32 = jnp.float32
bf16 = jnp.bfloat16


def _tile(n, pref, mult):
    if n <= pref:
        return n
    t = (pref // mult) * mult
    while t > 0 and n % t:
        t -= mult
    assert t > 0, (n, pref, mult)
    return t


def _params(*sem):
    return pltpu.CompilerParams(dimension_semantics=sem, vmem_limit_bytes=V7X_VMEM_LIMIT_BYTES)


def _sigmoid(x):
    return 1.0 / (1.0 + jnp.exp(-x))


def _dot(a, b):
    return jnp.dot(a, b, preferred_element_type=f32)


def _dot_nt(a, b):
    return lax.dot_general(a, b, (((1,), (1,)), ((), ())), preferred_element_type=f32)


def _dot_tn(a, b):
    return lax.dot_general(a, b, (((0,), (0,)), ((), ())), preferred_element_type=f32)


def _split_bf16(x):
    hi = x.astype(bf16)
    lo = (x - hi.astype(f32)).astype(bf16)
    return hi, lo


def _ada_kernel(c_ref, w_ref, b_ref, o_ref):
    c = c_ref[...]
    s_hi, s_lo = _split_bf16(c * _sigmoid(c))
    w = w_ref[...].astype(bf16)
    o_ref[...] = _dot(s_hi, w) + _dot(s_lo, w) + b_ref[...]


def _ada_call(cvec, ada_w, ada_b3, layer):
    rows, d = cvec.shape
    n6 = ada_w.shape[2]
    tn = _tile(n6, 512, 128)
    out = pl.pallas_call(
        _ada_kernel,
        out_shape=jax.ShapeDtypeStruct((rows, n6), f32),
        grid=(n6 // tn,),
        in_specs=[
            pl.BlockSpec((rows, d), lambda j: (0, 0)),
            pl.BlockSpec((None, d, tn), lambda j: (layer, 0, j)),
            pl.BlockSpec((None, 1, tn), lambda j: (layer, 0, j)),
        ],
        out_specs=pl.BlockSpec((rows, tn), lambda j: (0, j)),
        compiler_params=_params("arbitrary"),
        name="ada_mod",
    )(cvec, ada_w, ada_b3)
    return out.reshape(rows, 6, d).transpose(1, 0, 2)[:, :, None, :]


def _norm_mod(x, g, shift, scale):
    ms = jnp.mean(x * x, axis=-1, keepdims=True)
    y = x * lax.rsqrt(ms + EPS) * g
    return y * (1.0 + scale) + shift


def _norm_mod_kernel(x_ref, g_ref, sh_ref, sc_ref, o_ref):
    o_ref[...] = _norm_mod(x_ref[...], g_ref[...], sh_ref[...], sc_ref[...]).astype(o_ref.dtype)


def _group_of(i, tiles_per_batch, n_batch):
    return jnp.minimum(i // tiles_per_batch, n_batch)


def _norm_mod_call(x_all, g_row, mods, kinds, rows, seq, n_batch):
    d = x_all.shape[1]
    tm = _tile(math.gcd(seq, rows), 256, 8)
    tpb = seq // tm
    k_sh, k_sc = kinds
    return pl.pallas_call(
        _norm_mod_kernel,
        out_shape=jax.ShapeDtypeStruct((rows, d), bf16),
        grid=(rows // tm,),
        in_specs=[
            pl.BlockSpec((tm, d), lambda i: (i, 0)),
            pl.BlockSpec((1, d), lambda i: (0, 0)),
            pl.BlockSpec((None, None, 1, d), lambda i: (k_sh, _group_of(i, tpb, n_batch), 0, 0)),
            pl.BlockSpec((None, None, 1, d), lambda i: (k_sc, _group_of(i, tpb, n_batch), 0, 0)),
        ],
        out_specs=pl.BlockSpec((tm, d), lambda i: (i, 0)),
        compiler_params=_params("parallel"),
        name="norm_mod",
    )(x_all, g_row, mods, mods)


def _route(logits_t, bias_col):
    n_exp, tm = logits_t.shape
    gs = n_exp // ROUTE_GROUPS
    neg = -jnp.inf
    scores = _sigmoid(logits_t)
    choice = scores + bias_col

    iota_g = lax.broadcasted_iota(jnp.int32, (gs, tm), 0)
    grp_rows = []
    for g in range(ROUTE_GROUPS):
        xg = choice[g * gs:(g + 1) * gs, :]
        m1 = jnp.max(xg, axis=0, keepdims=True)
        i1 = jnp.min(jnp.where(xg == m1, iota_g, gs), axis=0, keepdims=True)
        m2 = jnp.max(jnp.where(iota_g == i1, neg, xg), axis=0, keepdims=True)
        grp_rows.append(m1 + m2)
    gsc = jnp.concatenate(grp_rows, axis=0)

    iota_gr = lax.broadcasted_iota(jnp.int32, (ROUTE_GROUPS, tm), 0)
    gsel = jnp.zeros((ROUTE_GROUPS, tm), f32)
    for _ in range(ROUTE_TOPK_GROUPS):
        mx = jnp.max(gsc, axis=0, keepdims=True)
        ix = jnp.min(jnp.where(gsc == mx, iota_gr, ROUTE_GROUPS), axis=0, keepdims=True)
        hit = iota_gr == ix
        gsel = jnp.where(hit, 1.0, gsel)
        gsc = jnp.where(hit, neg, gsc)
    emask = jnp.concatenate(
        [jnp.broadcast_to(gsel[g:g + 1, :], (gs, tm)) for g in range(ROUTE_GROUPS)], axis=0)

    masked = jnp.where(emask > 0, choice, neg)
    iota_e = lax.broadcasted_iota(jnp.int32, (n_exp, tm), 0)
    sel = jnp.zeros((n_exp, tm), f32)
    for _ in range(ROUTE_TOP_K):
        mx = jnp.max(masked, axis=0, keepdims=True)
        ix = jnp.min(jnp.where(masked == mx, iota_e, n_exp), axis=0, keepdims=True)
        hit = iota_e == ix
        sel = jnp.where(hit, 1.0, sel)
        masked = jnp.where(hit, neg, masked)
    w = jnp.where(sel > 0, scores, 0.0)
    return w / jnp.sum(w, axis=0, keepdims=True) * ROUTED_SCALE


def _norm_mod_router_kernel(x_ref, g_ref, sh_ref, sc_ref, rwh_ref, rwl_ref, rb_ref, m_ref, comb_ref):
    y = _norm_mod(x_ref[...], g_ref[...], sh_ref[...], sc_ref[...])
    m_ref[...] = y.astype(m_ref.dtype)
    y_hi, y_lo = _split_bf16(y)
    rwh = rwh_ref[...]
    logits_t = _dot_nt(rwh, y_hi) + _dot_nt(rwh, y_lo) + _dot_nt(rwl_ref[...], y_hi)
    comb_ref[...] = _route(logits_t, rb_ref[...])


def _norm_mod_router_call(x_all, g_row, mods, kinds, rows, seq, n_batch, rw_t_hi, rw_t_lo, rb_col):
    d = x_all.shape[1]
    n_exp = rw_t_hi.shape[0]
    tm = _tile(math.gcd(seq, rows), 256, 128)
    tpb = seq // tm
    k_sh, k_sc = kinds
    return pl.pallas_call(
        _norm_mod_router_kernel,
        out_shape=(jax.ShapeDtypeStruct((rows, d), bf16), jax.ShapeDtypeStruct((n_exp, rows), f32)),
        grid=(rows // tm,),
        in_specs=[
            pl.BlockSpec((tm, d), lambda i: (i, 0)),
            pl.BlockSpec((1, d), lambda i: (0, 0)),
            pl.BlockSpec((None, None, 1, d), lambda i: (k_sh, _group_of(i, tpb, n_batch), 0, 0)),
            pl.BlockSpec((None, None, 1, d), lambda i: (k_sc, _group_of(i, tpb, n_batch), 0, 0)),
            pl.BlockSpec((n_exp, d), lambda i: (0, 0)),
            pl.BlockSpec((n_exp, d), lambda i: (0, 0)),
            pl.BlockSpec((n_exp, 1), lambda i: (0, 0)),
        ],
        out_specs=(pl.BlockSpec((tm, d), lambda i: (i, 0)), pl.BlockSpec((n_exp, tm), lambda i: (0, i))),
        compiler_params=_params("parallel"),
        name="norm_mod_router",
    )(x_all, g_row, mods, mods, rw_t_hi, rw_t_lo, rb_col)


def _mm_kernel(a_ref, w_ref, o_ref):
    o_ref[...] = _dot(a_ref[...], w_ref[...]).astype(o_ref.dtype)


def _mm_call(a, w, out_dtype, rows=None, tm_pref=512, tn_pref=1024):
    rows = a.shape[0] if rows is None else rows
    k = a.shape[1]
    n = w.shape[1]
    tm = _tile(rows, tm_pref, 8)
    tn = _tile(n, tn_pref, 128)
    return pl.pallas_call(
        _mm_kernel,
        out_shape=jax.ShapeDtypeStruct((rows, n), out_dtype),
        grid=(rows // tm, n // tn),
        in_specs=[pl.BlockSpec((tm, k), lambda i, j: (i, 0)), pl.BlockSpec((k, tn), lambda i, j: (0, j))],
        out_specs=pl.BlockSpec((tm, tn), lambda i, j: (i, j)),
        compiler_params=_params("parallel", "arbitrary"),
        name="matmul",
    )(a, w)


def _mm_resid_kernel(a_ref, w_ref, x_ref, gate_ref, o_ref):
    o_ref[...] = x_ref[...] + gate_ref[...] * _dot(a_ref[...], w_ref[...])


def _mm_resid_call(a, w, x_all, mods, kind, rows, seq, n_batch):
    k = a.shape[1]
    n = w.shape[1]
    tm = _tile(math.gcd(seq, rows), 512, 8)
    tn = _tile(n, 512, 128)
    tpb = seq // tm
    return pl.pallas_call(
        _mm_resid_kernel,
        out_shape=jax.ShapeDtypeStruct((rows, n), f32),
        grid=(rows // tm, n // tn),
        in_specs=[
            pl.BlockSpec((tm, k), lambda i, j: (i, 0)),
            pl.BlockSpec((k, tn), lambda i, j: (0, j)),
            pl.BlockSpec((tm, tn), lambda i, j: (i, j)),
            pl.BlockSpec((None, None, 1, tn), lambda i, j: (kind, _group_of(i, tpb, n_batch), 0, j)),
        ],
        out_specs=pl.BlockSpec((tm, tn), lambda i, j: (i, j)),
        compiler_params=_params("parallel", "arbitrary"),
        name="matmul_resid",
    )(a, w, x_all, mods)


def _dft_tables(n):
    idx = jnp.arange(n, dtype=jnp.int32)
    ang = ((idx[:, None] * idx[None, :]) % n).astype(f32) * (2.0 * np.pi / n)
    s = float(n) ** -0.5
    return (jnp.cos(ang) * s).astype(bf16), (jnp.sin(ang) * s).astype(bf16)


def _chan_dft_call(u, wcs, rows):
    d = u.shape[1]
    cg = d // FNET_GROUPS
    tm = _tile(rows, 512, 8)
    return pl.pallas_call(
        _mm_kernel,
        out_shape=jax.ShapeDtypeStruct((rows, 2 * d), bf16),
        grid=(rows // tm, FNET_GROUPS),
        in_specs=[pl.BlockSpec((tm, cg), lambda i, g: (i, g)), pl.BlockSpec((cg, 2 * cg), lambda i, g: (0, 0))],
        out_specs=pl.BlockSpec((tm, 2 * cg), lambda i, g: (i, g)),
        compiler_params=_params("parallel", "arbitrary"),
        name="fnet_chan_dft",
    )(u, wcs)


def _pos_dft_kernel(c_ref, s_ref, pq_ref, o_ref):
    cg = o_ref.shape[-1]
    acc = _dot(c_ref[...], pq_ref[:, :cg]) - _dot(s_ref[...], pq_ref[:, cg:])
    o_ref[...] = acc.astype(o_ref.dtype)


def _pos_dft_call(pq, cos_t, sin_t, seq, n_batch, first_block):
    d = pq.shape[1] // 2
    cg = d // FNET_GROUPS
    tm = _tile(seq, 512, 8)
    return pl.pallas_call(
        _pos_dft_kernel,
        out_shape=jax.ShapeDtypeStruct((n_batch * seq, d), bf16),
        grid=(n_batch, FNET_GROUPS, seq // tm),
        in_specs=[
            pl.BlockSpec((tm, seq), lambda b, g, i: (i, 0)),
            pl.BlockSpec((tm, seq), lambda b, g, i: (i, 0)),
            pl.BlockSpec((seq, 2 * cg), lambda b, g, i: (first_block + b, g)),
        ],
        out_specs=pl.BlockSpec((tm, cg), lambda b, g, i: (b * (seq // tm) + i, g)),
        compiler_params=_params("parallel", "parallel", "arbitrary"),
        name="fnet_pos_dft",
    )(cos_t, sin_t, pq)


def _moe_kernel(m_ref, wg_ref, wu_ref, wd_ref, c_ref, o_ref):
    e = pl.program_id(1)
    m = m_ref[...]
    hg = _dot(m, wg_ref[...])
    hu = _dot(m, wu_ref[...])
    act = (hg * _sigmoid(hg)) * hu * c_ref[...]
    y = _dot(act.astype(bf16), wd_ref[...])

    @pl.when(e == 0)
    def _():
        o_ref[...] = y

    @pl.when(e > 0)
    def _():
        o_ref[...] += y


def _moe_call(m, wg, wu, wd, comb_col, rows):
    d = m.shape[1]
    n_exp, _, fdim = wg.shape
    tm = _tile(rows, 512, 8)
    return pl.pallas_call(
        _moe_kernel,
        out_shape=jax.ShapeDtypeStruct((rows, d), f32),
        grid=(rows // tm, n_exp),
        in_specs=[
            pl.BlockSpec((tm, d), lambda i, e: (i, 0)),
            pl.BlockSpec((None, d, fdim), lambda i, e: (e, 0, 0)),
            pl.BlockSpec((None, d, fdim), lambda i, e: (e, 0, 0)),
            pl.BlockSpec((None, fdim, d), lambda i, e: (e, 0, 0)),
            pl.BlockSpec((None, tm, 1), lambda i, e: (e, i, 0)),
        ],
        out_specs=pl.BlockSpec((tm, d), lambda i, e: (i, 0)),
        compiler_params=_params("parallel", "arbitrary"),
        name="moe_experts",
    )(m, wg, wu, wd, comb_col)


def _resid_kernel(x_ref, y_ref, gate_ref, o_ref):
    o_ref[...] = x_ref[...] + gate_ref[...] * y_ref[...]


def _resid_final_kernel(x_ref, y_ref, gate_ref, g_ref, o_ref):
    x = x_ref[...] + gate_ref[...] * y_ref[...]
    ms = jnp.mean(x * x, axis=-1, keepdims=True)
    o_ref[...] = x * lax.rsqrt(ms + EPS) * g_ref[...]


def _resid_call(x_all, y, mods, kind, rows, seq, n_batch, final_g_row=None):
    d = x_all.shape[1]
    tm = _tile(math.gcd(seq, rows), 256, 8)
    tpb = seq // tm
    in_specs = [
        pl.BlockSpec((tm, d), lambda i: (i, 0)),
        pl.BlockSpec((tm, d), lambda i: (i, 0)),
        pl.BlockSpec((None, None, 1, d), lambda i: (kind, _group_of(i, tpb, n_batch), 0, 0)),
    ]
    args = [x_all, y, mods]
    body = _resid_kernel
    if final_g_row is not None:
        in_specs.append(pl.BlockSpec((1, d), lambda i: (0, 0)))
        args.append(final_g_row)
        body = _resid_final_kernel
    return pl.pallas_call(
        body,
        out_shape=jax.ShapeDtypeStruct((rows, d), f32),
        grid=(rows // tm,),
        in_specs=in_specs,
        out_specs=pl.BlockSpec((tm, d), lambda i: (i, 0)),
        compiler_params=_params("parallel"),
        name="residual",
    )(*args)


def _gla_chunk(q, k, v, b, bx, st, reverse):
    ln, hd = q.shape
    c = GLA_BLOCK
    nb = ln // c

    def row(a, r):
        return a[r:r + 1, :]

    def bcast(rows):
        return jnp.concatenate([jnp.broadcast_to(r, (c, hd)) for r in rows], axis=0)

    beta = [row(bx, c * i + (c - 1 if reverse else 0)) for i in range(nb)]
    bend = [row(b, c * i + (0 if reverse else c - 1)) for i in range(nb)]
    btot = row(b, 0 if reverse else ln - 1)
    pos = (lambda i: nb - 1 - i) if reverse else (lambda i: i)
    blk_at = pos

    bstart = bcast(beta)
    e_q = jnp.exp(b - bstart)
    e_kd = jnp.exp(bstart - b)
    e_ke = jnp.exp(bcast(bend) - b)
    qe = q * e_q
    ke = k * e_ke

    lhs = [qe]
    for dist in range(2, nb):
        gap = []
        for i in range(nb):
            p = pos(i) - dist
            gap.append(jnp.exp(beta[i] - bend[blk_at(p)]) if p >= 0 else jnp.ones((1, hd), f32))
        lhs.append(qe * bcast(gap))
    p_off = _dot_nt(jnp.concatenate(lhs, axis=0).astype(bf16), ke.astype(bf16))
    p_diag = _dot_nt(qe.astype(bf16), (k * e_kd).astype(bf16))

    ti = lax.broadcasted_iota(jnp.int32, (ln, ln), 0)
    si = lax.broadcasted_iota(jnp.int32, (ln, ln), 1)
    tb, sb = ti // c, si // c
    dist_blk = (sb - tb) if reverse else (tb - sb)
    causal = (si >= ti) if reverse else (si <= ti)
    a = jnp.where((dist_blk == 0) & causal, p_diag, 0.0)
    for dist in range(1, nb):
        a = jnp.where(dist_blk == dist, p_off[(dist - 1) * ln:dist * ln, :], a)

    o = _dot(a.astype(bf16), v.astype(bf16))
    q_in = qe * jnp.exp(bstart)
    o = o + _dot_nt(q_in.astype(bf16), st.astype(bf16))
    khat = ke * bcast([jnp.exp(btot - bend[i]) for i in range(nb)])
    st_new = st * jnp.exp(btot) + _dot_tn(v.astype(bf16), khat.astype(bf16))
    return o, st_new


def _gla_kernel(q_ref, f_ref, v_ref, lb_ref, tri_ref, o_ref, st_ref, *, reverse):
    @pl.when(pl.program_id(2) == 0)
    def _():
        st_ref[...] = jnp.zeros_like(st_ref)

    rows, width = q_ref.shape
    hd = HEAD_DIM
    lb = lb_ref[...]
    f = lb + (1.0 - lb) * _sigmoid(f_ref[...].astype(f32))
    lf = jnp.log(f)
    lf_hi, lf_lo = _split_bf16(lf)
    tri = tri_ref[...]
    b = _dot(tri, lf_hi) + _dot(tri, lf_lo)
    bx = b - lf
    kk = 1.0 - f
    qq = q_ref[...].astype(f32) * (float(hd) ** -0.5)
    vv = v_ref[...].astype(f32)

    n_chunks = rows // GLA_CHUNK
    order = range(n_chunks - 1, -1, -1) if reverse else range(n_chunks)
    for h in range(width // hd):
        cs = slice(h * hd, (h + 1) * hd)
        st = st_ref[h]
        for ci in order:
            rs = slice(ci * GLA_CHUNK, (ci + 1) * GLA_CHUNK)
            o, st = _gla_chunk(qq[rs, cs], kk[rs, cs], vv[rs, cs], b[rs, cs], bx[rs, cs], st, reverse)
            o_ref[rs, cs] = o
        st_ref[h] = st


def _gla_call(proj, lb_row, tri, f_section, seq, ctx_len, n_batch, reverse):
    d = proj.shape[1] // 5
    r = GLA_STEP_ROWS
    w = GLA_HEADS_PER_STEP * HEAD_DIM
    nl, nc = seq // r, ctx_len // r
    ctx_base = n_batch * nl
    wb = d // w

    def row_block(bi, s):
        cs = jnp.minimum(s, nc - 1)
        ls = jnp.maximum(s - nc, 0)
        if reverse:
            cs, ls = nc - 1 - cs, nl - 1 - ls
        return jnp.where(s < nc, ctx_base + bi * nc + cs, bi * nl + ls)

    def out_block(bi, s):
        ls = jnp.maximum(s - nc, 0)
        if reverse:
            ls = nl - 1 - ls
        return bi * nl + ls

    def sec(k):
        return pl.BlockSpec((r, w), lambda bi, hb, s: (row_block(bi, s), k * wb + hb))

    return pl.pallas_call(
        functools.partial(_gla_kernel, reverse=reverse),
        out_shape=jax.ShapeDtypeStruct((n_batch * seq, d), f32),
        grid=(n_batch, wb, nc + nl),
        in_specs=[
            sec(0), sec(f_section), sec(3),
            pl.BlockSpec((1, w), lambda bi, hb, s: (0, hb)),
            pl.BlockSpec((r, r), lambda bi, hb, s: (0, 0)),
        ],
        out_specs=pl.BlockSpec((r, w), lambda bi, hb, s: (out_block(bi, s), hb)),
        scratch_shapes=[pltpu.VMEM((GLA_HEADS_PER_STEP, HEAD_DIM, HEAD_DIM), f32)],
        compiler_params=_params("parallel", "parallel", "arbitrary"),
        name="gla_scan_rev" if reverse else "gla_scan_fwd",
    )(proj, proj, proj, lb_row, tri)


def _chunk_tri(rows, reverse):
    t = np.arange(rows)[:, None]
    s = np.arange(rows)[None, :]
    same = (t // GLA_CHUNK) == (s // GLA_CHUNK)
    tri = same & ((s >= t) if reverse else (s <= t))
    return jnp.asarray(tri, dtype=bf16)


def _readout_kernel(of_ref, ob_ref, g_ref, gn_ref, o_ref):
    hd = HEAD_DIM
    gn = gn_ref[...]
    for h in range(o_ref.shape[-1] // hd):
        cs = slice(h * hd, (h + 1) * hd)
        o = of_ref[:, cs] + ob_ref[:, cs]
        g = g_ref[:, cs].astype(f32)
        y = o * lax.rsqrt(jnp.mean(o * o, axis=-1, keepdims=True) + EPS) * gn
        o_ref[:, cs] = (y * (g * _sigmoid(g))).astype(o_ref.dtype)


def _readout_call(o_f, o_b, proj, gn_row, rows):
    d = o_f.shape[1]
    tm = _tile(rows, 256, 8)
    w = _tile(d, 512, HEAD_DIM)
    wb = d // w
    return pl.pallas_call(
        _readout_kernel,
        out_shape=jax.ShapeDtypeStruct((rows, d), bf16),
        grid=(rows // tm, wb),
        in_specs=[
            pl.BlockSpec((tm, w), lambda i, j: (i, j)),
            pl.BlockSpec((tm, w), lambda i, j: (i, j)),
            pl.BlockSpec((tm, w), lambda i, j: (i, 4 * wb + j)),
            pl.BlockSpec((1, HEAD_DIM), lambda i, j: (0, 0)),
        ],
        out_specs=pl.BlockSpec((tm, w), lambda i, j: (i, j)),
        compiler_params=_params("parallel", "arbitrary"),
        name="hgrn_readout",
    )(o_f, o_b, proj, gn_row)


def kernel(x, c, ctx, c_ctx, ada_w, ada_b, norm1_g, norm2_g, fnet_w_in, fnet_w_out, hgrn_w_in, hgrn_lb,
           hgrn_gnorm_g, hgrn_w_out, router_w, router_b, exp_w_gate, exp_w_up, exp_w_down, sh_w_gate,
           sh_w_up, sh_w_down, final_g):
    n_batch, seq, d = x.shape
    ctx_len = ctx.shape[1]
    depth = ada_w.shape[0]
    assert depth == 2 and n_batch + 1 <= MOD_ROWS
    assert seq % GLA_STEP_ROWS == 0 and ctx_len % GLA_STEP_ROWS == 0 and seq % ctx_len == 0
    n_lat, n_ctx = n_batch * seq, n_batch * ctx_len
    n_all = n_lat + n_ctx
    fdim = exp_w_gate.shape[-1]
    n_sh = sh_w_gate.shape[-1] // fdim

    lb_all = jnp.cumsum(jax.nn.softmax(hgrn_lb.astype(f32), axis=0), axis=0)
    lb_all = lb_all - lb_all[0]

    x_all = jnp.concatenate([x.reshape(n_lat, d), ctx.reshape(n_ctx, d)], axis=0)
    cvec = jnp.concatenate([c, c_ctx[None, :], jnp.zeros((MOD_ROWS - n_batch - 1, d), f32)], axis=0)
    ada_b3 = ada_b[:, None, :]

    def moe_weights(i):
        def stack(we, ws):
            ws = ws.reshape(d, n_sh, fdim).transpose(1, 0, 2)
            return jnp.concatenate([we.astype(bf16), ws.astype(bf16)], axis=0)
        wd = jnp.concatenate([exp_w_down[i].astype(bf16), sh_w_down[i].reshape(n_sh, fdim, d).astype(bf16)], axis=0)
        return stack(exp_w_gate[i], sh_w_gate[i]), stack(exp_w_up[i], sh_w_up[i]), wd

    def channel_mixer(i, x_in, mods, rows, last):
        rw_t = router_w[i].T
        rw_hi, rw_lo = _split_bf16(rw_t)
        m, comb_t = _norm_mod_router_call(x_in, norm2_g[i][None, :], mods, (3, 4), rows, seq, n_batch,
                                          rw_hi, rw_lo, router_b[i][:, None])
        comb_col = jnp.concatenate([comb_t, jnp.ones((n_sh, rows), f32)], axis=0)[:, :, None]
        wg, wu, wd = moe_weights(i)
        y = _moe_call(m, wg, wu, wd, comb_col, rows)
        return _resid_call(x_in, y, mods, 5, rows, seq, n_batch, final_g[None, :] if last else None)

    mods = _ada_call(cvec, ada_w, ada_b3, 0)
    a = _norm_mod_call(x_all, norm1_g[0][None, :], mods, (0, 1), n_all, seq, n_batch)
    u = _mm_call(a, fnet_w_in[0].astype(bf16), bf16)
    cg = d // FNET_GROUPS
    cc, sc = _dft_tables(cg)
    pq = _chan_dft_call(u, jnp.concatenate([cc, sc], axis=1), n_all)
    z = jnp.concatenate([
        _pos_dft_call(pq, *_dft_tables(seq), seq, n_batch, 0),
        _pos_dft_call(pq, *_dft_tables(ctx_len), ctx_len, n_batch, n_lat // ctx_len),
    ], axis=0)
    x_all = _mm_resid_call(z, fnet_w_out[0].astype(bf16), x_all, mods, 2, n_all, seq, n_batch)
    x_all = channel_mixer(0, x_all, mods, n_all, False)

    mods = _ada_call(cvec, ada_w, ada_b3, 1)
    a = _norm_mod_call(x_all, norm1_g[1][None, :], mods, (0, 1), n_all, seq, n_batch)
    proj = _mm_call(a, hgrn_w_in[0].astype(bf16), bf16)
    o_f = _gla_call(proj, lb_all[1, 0][None, :], _chunk_tri(GLA_STEP_ROWS, False), 1, seq, ctx_len, n_batch, False)
    o_b = _gla_call(proj, lb_all[1, 1][None, :], _chunk_tri(GLA_STEP_ROWS, True), 2, seq, ctx_len, n_batch, True)
    r = _readout_call(o_f, o_b, proj, hgrn_gnorm_g[0][None, :], n_lat)
    x_lat = _mm_resid_call(r, hgrn_w_out[0].astype(bf16), x_all, mods, 2, n_lat, seq, n_batch)
    out = channel_mixer(1, x_lat, mods, n_lat, True)
    return out.reshape(n_batch, seq, d)
```

```python
import functools
import math

import jax
import jax.numpy as jnp
import numpy as np
from jax import lax
from jax.experimental import pallas as pl
from jax.experimental.pallas import tpu as pltpu

EPS = 1e-6
FNET_GROUPS = 8
HEAD_DIM = 128
ROUTE_GROUPS = 8
ROUTE_TOPK_GROUPS = 4
ROUTE_TOP_K = 8
ROUTED_SCALE = 2.5
GLA_CHUNK = 64
GLA_BLOCK = 16
GLA_STEP_ROWS = 256
GLA_HEADS_PER_STEP = 2
MOD_ROWS = 8
MOE_TILE_ROWS = 256

V7X_VMEM_LIMIT_BYTES = 56 * 1024 * 1024

f32 = jnp.float32
bf16 = jnp.bfloat16


def _tile(n, pref, mult):
    if n <= pref:
        return n
    t = (pref // mult) * mult
    while t > 0 and n % t:
        t -= mult
    assert t > 0, (n, pref, mult)
    return t


def _params(*sem):
    return pltpu.CompilerParams(dimension_semantics=sem, vmem_limit_bytes=V7X_VMEM_LIMIT_BYTES)


def _sigmoid(x):
    return 1.0 / (1.0 + jnp.exp(-x))


def _dot(a, b):
    return jnp.dot(a, b, preferred_element_type=f32)


def _dot_nt(a, b):
    return lax.dot_general(a, b, (((1,), (1,)), ((), ())), preferred_element_type=f32)


def _dot_tn(a, b):
    return lax.dot_general(a, b, (((0,), (0,)), ((), ())), preferred_element_type=f32)


def _split_bf16(x):
    hi = x.astype(bf16)
    lo = (x - hi.astype(f32)).astype(bf16)
    return hi, lo


def _ada_kernel(c_ref, w_ref, b_ref, o_ref):
    c = c_ref[...]
    s_hi, s_lo = _split_bf16(c * _sigmoid(c))
    w = w_ref[...].astype(bf16)
    o_ref[...] = _dot(s_hi, w) + _dot(s_lo, w) + b_ref[...]


def _ada_call(cvec, ada_w, ada_b3, layer):
    rows, d = cvec.shape
    n6 = ada_w.shape[2]
    tn = _tile(n6, 512, 128)
    out = pl.pallas_call(
        _ada_kernel,
        out_shape=jax.ShapeDtypeStruct((rows, n6), f32),
        grid=(n6 // tn,),
        in_specs=[
            pl.BlockSpec((rows, d), lambda j: (0, 0)),
            pl.BlockSpec((None, d, tn), lambda j: (layer, 0, j)),
            pl.BlockSpec((None, 1, tn), lambda j: (layer, 0, j)),
        ],
        out_specs=pl.BlockSpec((rows, tn), lambda j: (0, j)),
        compiler_params=_params("arbitrary"),
        name="ada_mod",
    )(cvec, ada_w, ada_b3)
    return out.reshape(rows, 6, d).transpose(1, 0, 2)[:, :, None, :]


def _norm_mod(x, g, shift, scale):
    ms = jnp.mean(x * x, axis=-1, keepdims=True)
    y = x * lax.rsqrt(ms + EPS) * g
    return y * (1.0 + scale) + shift


def _norm_mod_kernel(x_ref, g_ref, sh_ref, sc_ref, o_ref):
    o_ref[...] = _norm_mod(x_ref[...], g_ref[...], sh_ref[...], sc_ref[...]).astype(o_ref.dtype)


def _group_of(i, tiles_per_batch, n_batch):
    return jnp.minimum(i // tiles_per_batch, n_batch)


def _norm_mod_call(x_all, g_row, mods, kinds, rows, seq, n_batch):
    d = x_all.shape[1]
    tm = _tile(math.gcd(seq, rows), 256, 8)
    tpb = seq // tm
    k_sh, k_sc = kinds
    return pl.pallas_call(
        _norm_mod_kernel,
        out_shape=jax.ShapeDtypeStruct((rows, d), bf16),
        grid=(rows // tm,),
        in_specs=[
            pl.BlockSpec((tm, d), lambda i: (i, 0)),
            pl.BlockSpec((1, d), lambda i: (0, 0)),
            pl.BlockSpec((None, None, 1, d), lambda i: (k_sh, _group_of(i, tpb, n_batch), 0, 0)),
            pl.BlockSpec((None, None, 1, d), lambda i: (k_sc, _group_of(i, tpb, n_batch), 0, 0)),
        ],
        out_specs=pl.BlockSpec((tm, d), lambda i: (i, 0)),
        compiler_params=_params("parallel"),
        name="norm_mod",
    )(x_all, g_row, mods, mods)


def _route(logits_t, bias_col, carry, before):
    n_exp, tm = logits_t.shape
    gs = n_exp // ROUTE_GROUPS
    neg = -jnp.inf
    scores = _sigmoid(logits_t)
    choice = scores + bias_col

    iota_g = lax.broadcasted_iota(jnp.int32, (gs, tm), 0)
    grp_rows = []
    for g in range(ROUTE_GROUPS):
        xg = choice[g * gs:(g + 1) * gs, :]
        m1 = jnp.max(xg, axis=0, keepdims=True)
        i1 = jnp.min(jnp.where(xg == m1, iota_g, gs), axis=0, keepdims=True)
        m2 = jnp.max(jnp.where(iota_g == i1, neg, xg), axis=0, keepdims=True)
        grp_rows.append(m1 + m2)
    gsc = jnp.concatenate(grp_rows, axis=0)

    iota_gr = lax.broadcasted_iota(jnp.int32, (ROUTE_GROUPS, tm), 0)
    gsel = jnp.zeros((ROUTE_GROUPS, tm), f32)
    for _ in range(ROUTE_TOPK_GROUPS):
        mx = jnp.max(gsc, axis=0, keepdims=True)
        ix = jnp.min(jnp.where(gsc == mx, iota_gr, ROUTE_GROUPS), axis=0, keepdims=True)
        hit = iota_gr == ix
        gsel = jnp.where(hit, 1.0, gsel)
        gsc = jnp.where(hit, neg, gsc)
    emask = jnp.concatenate(
        [jnp.broadcast_to(gsel[g:g + 1, :], (gs, tm)) for g in range(ROUTE_GROUPS)], axis=0)

    masked = jnp.where(emask > 0, choice, neg)
    iota_e = lax.broadcasted_iota(jnp.int32, (n_exp, tm), 0)
    sel = jnp.zeros((n_exp, tm), f32)
    hits, e_rows, w_rows = [], [], []
    for _ in range(ROUTE_TOP_K):
        mx = jnp.max(masked, axis=0, keepdims=True)
        ix = jnp.min(jnp.where(masked == mx, iota_e, n_exp), axis=0, keepdims=True)
        hit = iota_e == ix
        sel = jnp.where(hit, 1.0, sel)
        masked = jnp.where(hit, neg, masked)
        hits.append(hit)
        e_rows.append(ix)
        w_rows.append(jnp.sum(jnp.where(hit, scores, 0.0), axis=0, keepdims=True))
    pos = _dot(sel.astype(bf16), before) + carry
    p_rows = [jnp.sum(jnp.where(hit, pos, 0.0), axis=0, keepdims=True) for hit in hits]
    w = jnp.concatenate(w_rows, axis=0)
    w = w / jnp.sum(w, axis=0, keepdims=True) * ROUTED_SCALE
    new_carry = carry + jnp.sum(sel, axis=1, keepdims=True)
    return (jnp.concatenate(e_rows, axis=0), jnp.concatenate(p_rows, axis=0).astype(jnp.int32), w,
            new_carry)


def _norm_mod_router_kernel(x_ref, g_ref, sh_ref, sc_ref, rwh_ref, rwl_ref, rb_ref, before_ref,
                            m_ref, e_ref, p_ref, w_ref, cnt_ref, carry_ref):
    @pl.when(pl.program_id(0) == 0)
    def _():
        carry_ref[...] = jnp.zeros_like(carry_ref)

    y = _norm_mod(x_ref[...], g_ref[...], sh_ref[...], sc_ref[...])
    m_ref[...] = y
    y_hi, y_lo = _split_bf16(y)
    rwh = rwh_ref[...]
    logits_t = _dot_nt(rwh, y_hi) + _dot_nt(rwh, y_lo) + _dot_nt(rwl_ref[...], y_hi)
    e_idx, rank, w, carry = _route(logits_t, rb_ref[...], carry_ref[...], before_ref[...])
    e_ref[...] = e_idx
    p_ref[...] = rank
    w_ref[...] = w
    carry_ref[...] = carry
    cnt_ref[...] = carry.astype(jnp.int32)


def _norm_mod_router_call(x_all, g_row, mods, kinds, rows, seq, n_batch, rw_t_hi, rw_t_lo, rb_col):
    d = x_all.shape[1]
    n_exp = rw_t_hi.shape[0]
    tm = _tile(math.gcd(seq, rows), 256, 128)
    tpb = seq // tm
    k_sh, k_sc = kinds
    before = jnp.asarray(np.arange(tm)[:, None] < np.arange(tm)[None, :], dtype=bf16)
    top = pl.BlockSpec((ROUTE_TOP_K, tm), lambda i: (0, i))
    return pl.pallas_call(
        _norm_mod_router_kernel,
        out_shape=(jax.ShapeDtypeStruct((rows, d), f32),
                   jax.ShapeDtypeStruct((ROUTE_TOP_K, rows), jnp.int32),
                   jax.ShapeDtypeStruct((ROUTE_TOP_K, rows), jnp.int32),
                   jax.ShapeDtypeStruct((ROUTE_TOP_K, rows), f32),
                   jax.ShapeDtypeStruct((n_exp, 1), jnp.int32)),
        grid=(rows // tm,),
        in_specs=[
            pl.BlockSpec((tm, d), lambda i: (i, 0)),
            pl.BlockSpec((1, d), lambda i: (0, 0)),
            pl.BlockSpec((None, None, 1, d), lambda i: (k_sh, _group_of(i, tpb, n_batch), 0, 0)),
            pl.BlockSpec((None, None, 1, d), lambda i: (k_sc, _group_of(i, tpb, n_batch), 0, 0)),
            pl.BlockSpec((n_exp, d), lambda i: (0, 0)),
            pl.BlockSpec((n_exp, d), lambda i: (0, 0)),
            pl.BlockSpec((n_exp, 1), lambda i: (0, 0)),
            pl.BlockSpec((tm, tm), lambda i: (0, 0)),
        ],
        out_specs=(pl.BlockSpec((tm, d), lambda i: (i, 0)), top, top, top,
                   pl.BlockSpec((n_exp, 1), lambda i: (0, 0))),
        scratch_shapes=[pltpu.VMEM((n_exp, 1), f32)],
        compiler_params=_params("arbitrary"),
        name="norm_mod_router",
    )(x_all, g_row, mods, mods, rw_t_hi, rw_t_lo, rb_col, before)


def _mm_kernel(a_ref, w_ref, o_ref):
    o_ref[...] = _dot(a_ref[...], w_ref[...]).astype(o_ref.dtype)


def _mm_call(a, w, out_dtype, rows=None, tm_pref=512, tn_pref=1024):
    rows = a.shape[0] if rows is None else rows
    k = a.shape[1]
    n = w.shape[1]
    tm = _tile(rows, tm_pref, 8)
    tn = _tile(n, tn_pref, 128)
    return pl.pallas_call(
        _mm_kernel,
        out_shape=jax.ShapeDtypeStruct((rows, n), out_dtype),
        grid=(rows // tm, n // tn),
        in_specs=[pl.BlockSpec((tm, k), lambda i, j: (i, 0)), pl.BlockSpec((k, tn), lambda i, j: (0, j))],
        out_specs=pl.BlockSpec((tm, tn), lambda i, j: (i, j)),
        compiler_params=_params("parallel", "arbitrary"),
        name="matmul",
    )(a, w)


def _mm_resid_kernel(a_ref, w_ref, x_ref, gate_ref, o_ref):
    o_ref[...] = x_ref[...] + gate_ref[...] * _dot(a_ref[...], w_ref[...])


def _mm_resid_call(a, w, x_all, mods, kind, rows, seq, n_batch):
    k = a.shape[1]
    n = w.shape[1]
    tm = _tile(math.gcd(seq, rows), 512, 8)
    tn = _tile(n, 512, 128)
    tpb = seq // tm
    return pl.pallas_call(
        _mm_resid_kernel,
        out_shape=jax.ShapeDtypeStruct((rows, n), f32),
        grid=(rows // tm, n // tn),
        in_specs=[
            pl.BlockSpec((tm, k), lambda i, j: (i, 0)),
            pl.BlockSpec((k, tn), lambda i, j: (0, j)),
            pl.BlockSpec((tm, tn), lambda i, j: (i, j)),
            pl.BlockSpec((None, None, 1, tn), lambda i, j: (kind, _group_of(i, tpb, n_batch), 0, j)),
        ],
        out_specs=pl.BlockSpec((tm, tn), lambda i, j: (i, j)),
        compiler_params=_params("parallel", "arbitrary"),
        name="matmul_resid",
    )(a, w, x_all, mods)


def _dft_tables(n):
    idx = jnp.arange(n, dtype=jnp.int32)
    ang = ((idx[:, None] * idx[None, :]) % n).astype(f32) * (2.0 * np.pi / n)
    s = float(n) ** -0.5
    return (jnp.cos(ang) * s).astype(bf16), (jnp.sin(ang) * s).astype(bf16)


def _chan_dft_call(u, wcs, rows):
    d = u.shape[1]
    cg = d // FNET_GROUPS
    tm = _tile(rows, 512, 8)
    return pl.pallas_call(
        _mm_kernel,
        out_shape=jax.ShapeDtypeStruct((rows, 2 * d), bf16),
        grid=(rows // tm, FNET_GROUPS),
        in_specs=[pl.BlockSpec((tm, cg), lambda i, g: (i, g)), pl.BlockSpec((cg, 2 * cg), lambda i, g: (0, 0))],
        out_specs=pl.BlockSpec((tm, 2 * cg), lambda i, g: (i, g)),
        compiler_params=_params("parallel", "arbitrary"),
        name="fnet_chan_dft",
    )(u, wcs)


def _pos_dft_kernel(c_ref, s_ref, pq_ref, o_ref):
    cg = o_ref.shape[-1]
    acc = _dot(c_ref[...], pq_ref[:, :cg]) - _dot(s_ref[...], pq_ref[:, cg:])
    o_ref[...] = acc.astype(o_ref.dtype)


def _pos_dft_call(pq, cos_t, sin_t, seq, n_batch, first_block):
    d = pq.shape[1] // 2
    cg = d // FNET_GROUPS
    tm = _tile(seq, 512, 8)
    return pl.pallas_call(
        _pos_dft_kernel,
        out_shape=jax.ShapeDtypeStruct((n_batch * seq, d), bf16),
        grid=(n_batch, FNET_GROUPS, seq // tm),
        in_specs=[
            pl.BlockSpec((tm, seq), lambda b, g, i: (i, 0)),
            pl.BlockSpec((tm, seq), lambda b, g, i: (i, 0)),
            pl.BlockSpec((seq, 2 * cg), lambda b, g, i: (first_block + b, g)),
        ],
        out_specs=pl.BlockSpec((tm, cg), lambda b, g, i: (b * (seq // tm) + i, g)),
        compiler_params=_params("parallel", "parallel", "arbitrary"),
        name="fnet_pos_dft",
    )(cos_t, sin_t, pq)


def _glu(x, wg, wu, wd):
    hg = _dot(x, wg)
    hu = _dot(x, wu)
    return _dot(((hg * _sigmoid(hg)) * hu).astype(bf16), wd)


def _shared_kernel(m_ref, wg_ref, wu_ref, wd_ref, o_ref):
    j = pl.program_id(1)
    y = _glu(m_ref[...].astype(bf16), wg_ref[...], wu_ref[...], wd_ref[...])

    @pl.when(j == 0)
    def _():
        o_ref[...] = y

    @pl.when(j > 0)
    def _():
        o_ref[...] += y


def _shared_call(m, wg, wu, wd, rows):
    d = m.shape[1]
    n_chunk, _, fdim = wg.shape
    tm = _tile(rows, 512, 8)
    return pl.pallas_call(
        _shared_kernel,
        out_shape=jax.ShapeDtypeStruct((rows, d), f32),
        grid=(rows // tm, n_chunk),
        in_specs=[
            pl.BlockSpec((tm, d), lambda i, j: (i, 0)),
            pl.BlockSpec((None, d, fdim), lambda i, j: (j, 0, 0)),
            pl.BlockSpec((None, d, fdim), lambda i, j: (j, 0, 0)),
            pl.BlockSpec((None, fdim, d), lambda i, j: (j, 0, 0)),
        ],
        out_specs=pl.BlockSpec((tm, d), lambda i, j: (i, 0)),
        compiler_params=_params("parallel", "arbitrary"),
        name="moe_shared",
    )(m, wg, wu, wd)


def _dispatch_plan(e_idx, rank, counts, rows, tile_rows):
    n_exp = counts.shape[0]
    n_tiles = (ROUTE_TOP_K * rows) // tile_rows + n_exp
    tiles_e = (counts + tile_rows - 1) // tile_rows
    tile_end = jnp.cumsum(tiles_e)
    first_slot = (tile_end - tiles_e) * tile_rows
    dest = first_slot[e_idx] + rank
    tile_expert = jnp.minimum(
        jnp.searchsorted(tile_end, jnp.arange(n_tiles, dtype=jnp.int32), side="right"), n_exp - 1)
    tokens = jnp.broadcast_to(jnp.arange(rows, dtype=jnp.int32)[None, :], dest.shape)
    src = jnp.zeros(((n_tiles + 1) * tile_rows,), jnp.int32).at[dest.reshape(-1)].set(tokens.reshape(-1))
    return n_tiles, tile_expert.astype(jnp.int32), src.reshape(n_tiles + 1, 1, tile_rows), dest


def _row_gather(idx_ref, n, src_hbm, dst_buf, sem):
    for r in range(n):
        pltpu.make_async_copy(src_hbm.at[pl.ds(idx_ref[0, r], 1), :], dst_buf.at[pl.ds(r, 1), :], sem).start()


def _row_gather_wait(n, src_hbm, dst_buf, sem):
    pltpu.make_async_copy(src_hbm.at[pl.ds(0, n), :], dst_buf, sem).wait()


def _expert_kernel(te_ref, src_cur_ref, src_nxt_ref, m_hbm, wg_ref, wu_ref, wd_ref, ys_ref,
                   xbuf, sem, wgb, wub, wdb):
    t = pl.program_id(0)
    n_t = pl.num_programs(0)
    tile_rows = ys_ref.shape[0]
    slot = t % 2

    @pl.when(t == 0)
    def _():
        _row_gather(src_cur_ref, tile_rows, m_hbm, xbuf.at[0], sem.at[0])

    @pl.when((t == 0) | (te_ref[t] != te_ref[jnp.maximum(t - 1, 0)]))
    def _():
        wgb[...] = wg_ref[...].astype(bf16)
        wub[...] = wu_ref[...].astype(bf16)
        wdb[...] = wd_ref[...].astype(bf16)

    _row_gather(src_nxt_ref, tile_rows, m_hbm, xbuf.at[1 - slot], sem.at[1 - slot])
    _row_gather_wait(tile_rows, m_hbm, xbuf.at[slot], sem.at[slot])
    ys_ref[...] = _glu(xbuf[slot].astype(bf16), wgb[...], wub[...], wdb[...])

    @pl.when(t == n_t - 1)
    def _():
        _row_gather_wait(tile_rows, m_hbm, xbuf.at[1 - slot], sem.at[1 - slot])


def _expert_call(m, wg, wu, wd, tile_expert, src, n_tiles, tile_rows):
    d = m.shape[1]
    fdim = wg.shape[-1]
    smem_tile = lambda off: pl.BlockSpec((None, 1, tile_rows), lambda t, te: (t + off, 0, 0),
                                         memory_space=pltpu.SMEM)
    grid_spec = pltpu.PrefetchScalarGridSpec(
        num_scalar_prefetch=1,
        grid=(n_tiles,),
        in_specs=[
            smem_tile(0), smem_tile(1),
            pl.BlockSpec(memory_space=pl.ANY),
            pl.BlockSpec((None, d, fdim), lambda t, te: (te[t], 0, 0)),
            pl.BlockSpec((None, d, fdim), lambda t, te: (te[t], 0, 0)),
            pl.BlockSpec((None, fdim, d), lambda t, te: (te[t], 0, 0)),
        ],
        out_specs=pl.BlockSpec((tile_rows, d), lambda t, te: (t, 0)),
        scratch_shapes=[
            pltpu.VMEM((2, tile_rows, d), f32),
            pltpu.SemaphoreType.DMA((2,)),
            pltpu.VMEM((d, fdim), bf16), pltpu.VMEM((d, fdim), bf16), pltpu.VMEM((fdim, d), bf16),
        ],
    )
    return pl.pallas_call(
        _expert_kernel,
        out_shape=jax.ShapeDtypeStruct((n_tiles * tile_rows, d), f32),
        grid_spec=grid_spec,
        compiler_params=_params("arbitrary"),
        name="moe_experts",
    )(tile_expert, src, src, m, wg, wu, wd)


def _combine_kernel(dst_cur_ref, dst_nxt_ref, ys_hbm, w_ref, x_ref, ysh_ref, gate_ref, *rest, final):
    if final:
        g_ref, o_ref, gbuf, sem = rest
    else:
        o_ref, gbuf, sem = rest
    i = pl.program_id(0)
    n_i = pl.num_programs(0)
    tm = o_ref.shape[0]
    n_rows = ROUTE_TOP_K * tm
    slot = i % 2

    @pl.when(i == 0)
    def _():
        _row_gather(dst_cur_ref, n_rows, ys_hbm, gbuf.at[0], sem.at[0])

    _row_gather(dst_nxt_ref, n_rows, ys_hbm, gbuf.at[1 - slot], sem.at[1 - slot])
    _row_gather_wait(n_rows, ys_hbm, gbuf.at[slot], sem.at[slot])

    w = w_ref[...]
    acc = ysh_ref[...]
    for k in range(ROUTE_TOP_K):
        acc = acc + w[:, k:k + 1] * gbuf[slot, k * tm:(k + 1) * tm, :]
    x = x_ref[...] + gate_ref[...] * acc
    if final:
        x = x * lax.rsqrt(jnp.mean(x * x, axis=-1, keepdims=True) + EPS) * g_ref[...]
    o_ref[...] = x

    @pl.when(i == n_i - 1)
    def _():
        _row_gather_wait(n_rows, ys_hbm, gbuf.at[1 - slot], sem.at[1 - slot])


def _combine_call(ys, dest, w_t, x_all, y_sh, mods, kind, rows, seq, n_batch, final_g_row=None):
    d = x_all.shape[1]
    tm = _tile(math.gcd(seq, rows), 64, 8)
    tpb = seq // tm
    n_i = rows // tm
    dst = dest.reshape(ROUTE_TOP_K, n_i, tm).transpose(1, 0, 2).reshape(n_i, 1, ROUTE_TOP_K * tm)
    dst = jnp.concatenate([dst, jnp.zeros((1, 1, ROUTE_TOP_K * tm), jnp.int32)], axis=0)
    smem_tile = lambda off: pl.BlockSpec((None, 1, ROUTE_TOP_K * tm), lambda i: (i + off, 0, 0),
                                         memory_space=pltpu.SMEM)
    in_specs = [
        smem_tile(0), smem_tile(1),
        pl.BlockSpec(memory_space=pl.ANY),
        pl.BlockSpec((tm, ROUTE_TOP_K), lambda i: (i, 0)),
        pl.BlockSpec((tm, d), lambda i: (i, 0)),
        pl.BlockSpec((tm, d), lambda i: (i, 0)),
        pl.BlockSpec((None, None, 1, d), lambda i: (kind, _group_of(i, tpb, n_batch), 0, 0)),
    ]
    args = [dst, dst, ys, w_t, x_all, y_sh, mods]
    if final_g_row is not None:
        in_specs.append(pl.BlockSpec((1, d), lambda i: (0, 0)))
        args.append(final_g_row)
    return pl.pallas_call(
        functools.partial(_combine_kernel, final=final_g_row is not None),
        out_shape=jax.ShapeDtypeStruct((rows, d), f32),
        grid=(n_i,),
        in_specs=in_specs,
        out_specs=pl.BlockSpec((tm, d), lambda i: (i, 0)),
        scratch_shapes=[pltpu.VMEM((2, ROUTE_TOP_K * tm, d), f32), pltpu.SemaphoreType.DMA((2,))],
        compiler_params=_params("arbitrary"),
        name="moe_combine",
    )(*args)


def _gla_chunk(q, k, v, b, bx, st, reverse):
    ln, hd = q.shape
    c = GLA_BLOCK
    nb = ln // c

    def row(a, r):
        return a[r:r + 1, :]

    def bcast(rows):
        return jnp.concatenate([jnp.broadcast_to(r, (c, hd)) for r in rows], axis=0)

    beta = [row(bx, c * i + (c - 1 if reverse else 0)) for i in range(nb)]
    bend = [row(b, c * i + (0 if reverse else c - 1)) for i in range(nb)]
    btot = row(b, 0 if reverse else ln - 1)
    pos = (lambda i: nb - 1 - i) if reverse else (lambda i: i)
    blk_at = pos

    bstart = bcast(beta)
    e_q = jnp.exp(b - bstart)
    e_kd = jnp.exp(bstart - b)
    e_ke = jnp.exp(bcast(bend) - b)
    qe = q * e_q
    ke = k * e_ke

    lhs = [qe]
    for dist in range(2, nb):
        gap = []
        for i in range(nb):
            p = pos(i) - dist
            gap.append(jnp.exp(beta[i] - bend[blk_at(p)]) if p >= 0 else jnp.ones((1, hd), f32))
        lhs.append(qe * bcast(gap))
    p_off = _dot_nt(jnp.concatenate(lhs, axis=0).astype(bf16), ke.astype(bf16))
    p_diag = _dot_nt(qe.astype(bf16), (k * e_kd).astype(bf16))

    ti = lax.broadcasted_iota(jnp.int32, (ln, ln), 0)
    si = lax.broadcasted_iota(jnp.int32, (ln, ln), 1)
    tb, sb = ti // c, si // c
    dist_blk = (sb - tb) if reverse else (tb - sb)
    causal = (si >= ti) if reverse else (si <= ti)
    a = jnp.where((dist_blk == 0) & causal, p_diag, 0.0)
    for dist in range(1, nb):
        a = jnp.where(dist_blk == dist, p_off[(dist - 1) * ln:dist * ln, :], a)

    o = _dot(a.astype(bf16), v.astype(bf16))
    q_in = qe * jnp.exp(bstart)
    o = o + _dot_nt(q_in.astype(bf16), st.astype(bf16))
    khat = ke * bcast([jnp.exp(btot - bend[i]) for i in range(nb)])
    st_new = st * jnp.exp(btot) + _dot_tn(v.astype(bf16), khat.astype(bf16))
    return o, st_new


def _gla_kernel(q_ref, f_ref, v_ref, lb_ref, tri_ref, o_ref, st_ref, *, reverse):
    @pl.when(pl.program_id(2) == 0)
    def _():
        st_ref[...] = jnp.zeros_like(st_ref)

    rows, width = q_ref.shape
    hd = HEAD_DIM
    lb = lb_ref[...]
    f = lb + (1.0 - lb) * _sigmoid(f_ref[...].astype(f32))
    lf = jnp.log(f)
    lf_hi, lf_lo = _split_bf16(lf)
    tri = tri_ref[...]
    b = _dot(tri, lf_hi) + _dot(tri, lf_lo)
    bx = b - lf
    kk = 1.0 - f
    qq = q_ref[...].astype(f32) * (float(hd) ** -0.5)
    vv = v_ref[...].astype(f32)

    n_chunks = rows // GLA_CHUNK
    order = range(n_chunks - 1, -1, -1) if reverse else range(n_chunks)
    for h in range(width // hd):
        cs = slice(h * hd, (h + 1) * hd)
        st = st_ref[h]
        for ci in order:
            rs = slice(ci * GLA_CHUNK, (ci + 1) * GLA_CHUNK)
            o, st = _gla_chunk(qq[rs, cs], kk[rs, cs], vv[rs, cs], b[rs, cs], bx[rs, cs], st, reverse)
            o_ref[rs, cs] = o
        st_ref[h] = st


def _gla_call(proj, lb_row, tri, f_section, seq, ctx_len, n_batch, reverse):
    d = proj.shape[1] // 5
    r = GLA_STEP_ROWS
    w = GLA_HEADS_PER_STEP * HEAD_DIM
    nl, nc = seq // r, ctx_len // r
    ctx_base = n_batch * nl
    wb = d // w

    def row_block(bi, s):
        cs = jnp.minimum(s, nc - 1)
        ls = jnp.maximum(s - nc, 0)
        if reverse:
            cs, ls = nc - 1 - cs, nl - 1 - ls
        return jnp.where(s < nc, ctx_base + bi * nc + cs, bi * nl + ls)

    def out_block(bi, s):
        ls = jnp.maximum(s - nc, 0)
        if reverse:
            ls = nl - 1 - ls
        return bi * nl + ls

    def sec(k):
        return pl.BlockSpec((r, w), lambda bi, hb, s: (row_block(bi, s), k * wb + hb))

    return pl.pallas_call(
        functools.partial(_gla_kernel, reverse=reverse),
        out_shape=jax.ShapeDtypeStruct((n_batch * seq, d), f32),
        grid=(n_batch, wb, nc + nl),
        in_specs=[
            sec(0), sec(f_section), sec(3),
            pl.BlockSpec((1, w), lambda bi, hb, s: (0, hb)),
            pl.BlockSpec((r, r), lambda bi, hb, s: (0, 0)),
        ],
        out_specs=pl.BlockSpec((r, w), lambda bi, hb, s: (out_block(bi, s), hb)),
        scratch_shapes=[pltpu.VMEM((GLA_HEADS_PER_STEP, HEAD_DIM, HEAD_DIM), f32)],
        compiler_params=_params("parallel", "parallel", "arbitrary"),
        name="gla_scan_rev" if reverse else "gla_scan_fwd",
    )(proj, proj, proj, lb_row, tri)


def _chunk_tri(rows, reverse):
    t = np.arange(rows)[:, None]
    s = np.arange(rows)[None, :]
    same = (t // GLA_CHUNK) == (s // GLA_CHUNK)
    tri = same & ((s >= t) if reverse else (s <= t))
    return jnp.asarray(tri, dtype=bf16)


def _readout_kernel(of_ref, ob_ref, g_ref, gn_ref, o_ref):
    hd = HEAD_DIM
    gn = gn_ref[...]
    for h in range(o_ref.shape[-1] // hd):
        cs = slice(h * hd, (h + 1) * hd)
        o = of_ref[:, cs] + ob_ref[:, cs]
        g = g_ref[:, cs].astype(f32)
        y = o * lax.rsqrt(jnp.mean(o * o, axis=-1, keepdims=True) + EPS) * gn
        o_ref[:, cs] = (y * (g * _sigmoid(g))).astype(o_ref.dtype)


def _readout_call(o_f, o_b, proj, gn_row, rows):
    d = o_f.shape[1]
    tm = _tile(rows, 256, 8)
    w = _tile(d, 512, HEAD_DIM)
    wb = d // w
    return pl.pallas_call(
        _readout_kernel,
        out_shape=jax.ShapeDtypeStruct((rows, d), bf16),
        grid=(rows // tm, wb),
        in_specs=[
            pl.BlockSpec((tm, w), lambda i, j: (i, j)),
            pl.BlockSpec((tm, w), lambda i, j: (i, j)),
            pl.BlockSpec((tm, w), lambda i, j: (i, 4 * wb + j)),
            pl.BlockSpec((1, HEAD_DIM), lambda i, j: (0, 0)),
        ],
        out_specs=pl.BlockSpec((tm, w), lambda i, j: (i, j)),
        compiler_params=_params("parallel", "arbitrary"),
        name="hgrn_readout",
    )(o_f, o_b, proj, gn_row)


def kernel(x, c, ctx, c_ctx, ada_w, ada_b, norm1_g, norm2_g, fnet_w_in, fnet_w_out, hgrn_w_in, hgrn_lb,
           hgrn_gnorm_g, hgrn_w_out, router_w, router_b, exp_w_gate, exp_w_up, exp_w_down, sh_w_gate,
           sh_w_up, sh_w_down, final_g):
    n_batch, seq, d = x.shape
    ctx_len = ctx.shape[1]
    depth = ada_w.shape[0]
    assert depth == 2 and n_batch + 1 <= MOD_ROWS
    assert seq % GLA_STEP_ROWS == 0 and ctx_len % GLA_STEP_ROWS == 0 and seq % ctx_len == 0
    n_lat, n_ctx = n_batch * seq, n_batch * ctx_len
    n_all = n_lat + n_ctx
    fdim = exp_w_gate.shape[-1]
    n_sh = sh_w_gate.shape[-1] // fdim

    lb_all = jnp.cumsum(jax.nn.softmax(hgrn_lb.astype(f32), axis=0), axis=0)
    lb_all = lb_all - lb_all[0]

    x_all = jnp.concatenate([x.reshape(n_lat, d), ctx.reshape(n_ctx, d)], axis=0)
    cvec = jnp.concatenate([c, c_ctx[None, :], jnp.zeros((MOD_ROWS - n_batch - 1, d), f32)], axis=0)
    ada_b3 = ada_b[:, None, :]

    def channel_mixer(i, x_in, mods, rows, last):
        rw_hi, rw_lo = _split_bf16(router_w[i].T)
        m, e_idx, rank, w_k, counts = _norm_mod_router_call(
            x_in, norm2_g[i][None, :], mods, (3, 4), rows, seq, n_batch, rw_hi, rw_lo, router_b[i][:, None])
        n_tiles, tile_expert, src, dest = _dispatch_plan(e_idx, rank, counts[:, 0], rows, MOE_TILE_ROWS)
        ys = _expert_call(m, exp_w_gate[i], exp_w_up[i], exp_w_down[i], tile_expert, src, n_tiles,
                          MOE_TILE_ROWS)
        split_in = lambda ws: ws.astype(bf16).reshape(d, n_sh, fdim).transpose(1, 0, 2)
        y_sh = _shared_call(m, split_in(sh_w_gate[i]), split_in(sh_w_up[i]),
                            sh_w_down[i].astype(bf16).reshape(n_sh, fdim, d), rows)
        return _combine_call(ys, dest, w_k.T, x_in, y_sh, mods, 5, rows, seq, n_batch,
                             final_g[None, :] if last else None)

    mods = _ada_call(cvec, ada_w, ada_b3, 0)
    a = _norm_mod_call(x_all, norm1_g[0][None, :], mods, (0, 1), n_all, seq, n_batch)
    u = _mm_call(a, fnet_w_in[0].astype(bf16), bf16)
    cg = d // FNET_GROUPS
    cc, sc = _dft_tables(cg)
    pq = _chan_dft_call(u, jnp.concatenate([cc, sc], axis=1), n_all)
    z = jnp.concatenate([
        _pos_dft_call(pq, *_dft_tables(seq), seq, n_batch, 0),
        _pos_dft_call(pq, *_dft_tables(ctx_len), ctx_len, n_batch, n_lat // ctx_len),
    ], axis=0)
    x_all = _mm_resid_call(z, fnet_w_out[0].astype(bf16), x_all, mods, 2, n_all, seq, n_batch)
    x_all = channel_mixer(0, x_all, mods, n_all, False)

    mods = _ada_call(cvec, ada_w, ada_b3, 1)
    a = _norm_mod_call(x_all, norm1_g[1][None, :], mods, (0, 1), n_all, seq, n_batch)
    proj = _mm_call(a, hgrn_w_in[0].astype(bf16), bf16)
    o_f = _gla_call(proj, lb_all[1, 0][None, :], _chunk_tri(GLA_STEP_ROWS, False), 1, seq, ctx_len, n_batch, False)
    o_b = _gla_call(proj, lb_all[1, 1][None, :], _chunk_tri(GLA_STEP_ROWS, True), 2, seq, ctx_len, n_batch, True)
    r = _readout_call(o_f, o_b, proj, hgrn_gnorm_g[0][None, :], n_lat)
    x_lat = _mm_resid_call(r, hgrn_w_out[0].astype(bf16), x_all, mods, 2, n_lat, seq, n_batch)
    out = channel_mixer(1, x_lat, mods, n_lat, True)
    return out.reshape(n_batch, seq, d)
```

```python
import functools
import math

import jax
import jax.numpy as jnp
import numpy as np
from jax import lax
from jax.experimental import pallas as pl
from jax.experimental.pallas import tpu as pltpu

EPS = 1e-6
FNET_GROUPS = 8
HEAD_DIM = 128
ROUTE_GROUPS = 8
ROUTE_TOPK_GROUPS = 4
ROUTE_TOP_K = 8
ROUTED_SCALE = 2.5
GLA_CHUNK = 64
GLA_BLOCK = 16
GLA_STEP_ROWS = 256
GLA_HEADS_PER_STEP = 2
MOD_ROWS = 8
MOE_TILE_ROWS = 256
LANES = 128
SLAB_PITCH = 40

V7X_VMEM_LIMIT_BYTES = 56 * 1024 * 1024

f32 = jnp.float32
bf16 = jnp.bfloat16


def _tile(n, pref, mult):
    if n <= pref:
        return n
    t = (pref // mult) * mult
    while t > 0 and n % t:
        t -= mult
    assert t > 0, (n, pref, mult)
    return t


def _params(*sem):
    return pltpu.CompilerParams(dimension_semantics=sem, vmem_limit_bytes=V7X_VMEM_LIMIT_BYTES)


def _sigmoid(x):
    return 1.0 / (1.0 + jnp.exp(-x))


def _dot(a, b):
    return jnp.dot(a, b, preferred_element_type=f32)


def _dot_nt(a, b):
    return lax.dot_general(a, b, (((1,), (1,)), ((), ())), preferred_element_type=f32)


def _dot_tn(a, b):
    return lax.dot_general(a, b, (((0,), (0,)), ((), ())), preferred_element_type=f32)


def _split_bf16(x):
    hi = x.astype(bf16)
    lo = (x - hi.astype(f32)).astype(bf16)
    return hi, lo


def _store_slabs(ref, rows_val, pitch):
    n = rows_val.shape[0]
    for s in range(rows_val.shape[1] // LANES):
        ref[pl.ds(s, n, stride=pitch), :] = rows_val[:, s * LANES:(s + 1) * LANES]


def _load_slab_chunk(ref, first_row, n, s, pitch):
    return ref[pl.ds(first_row + s, n, stride=pitch), :]


def _ada_kernel(c_ref, w_ref, b_ref, o_ref):
    c = c_ref[...]
    s_hi, s_lo = _split_bf16(c * _sigmoid(c))
    w = w_ref[...].astype(bf16)
    o_ref[...] = _dot(s_hi, w) + _dot(s_lo, w) + b_ref[...]


def _ada_call(cvec, ada_w, ada_b3, layer):
    rows, d = cvec.shape
    n6 = ada_w.shape[2]
    tn = _tile(n6, 512, 128)
    out = pl.pallas_call(
        _ada_kernel,
        out_shape=jax.ShapeDtypeStruct((rows, n6), f32),
        grid=(n6 // tn,),
        in_specs=[
            pl.BlockSpec((rows, d), lambda j: (0, 0)),
            pl.BlockSpec((None, d, tn), lambda j: (layer, 0, j)),
            pl.BlockSpec((None, 1, tn), lambda j: (layer, 0, j)),
        ],
        out_specs=pl.BlockSpec((rows, tn), lambda j: (0, j)),
        compiler_params=_params("arbitrary"),
        name="ada_mod",
    )(cvec, ada_w, ada_b3)
    return out.reshape(rows, 6, d).transpose(1, 0, 2)[:, :, None, :]


def _norm_mod(x, g, shift, scale):
    ms = jnp.mean(x * x, axis=-1, keepdims=True)
    y = x * lax.rsqrt(ms + EPS) * g
    return y * (1.0 + scale) + shift


def _norm_mod_kernel(x_ref, g_ref, sh_ref, sc_ref, o_ref):
    o_ref[...] = _norm_mod(x_ref[...], g_ref[...], sh_ref[...], sc_ref[...]).astype(o_ref.dtype)


def _group_of(i, tiles_per_batch, n_batch):
    return jnp.minimum(i // tiles_per_batch, n_batch)


def _norm_mod_call(x_all, g_row, mods, kinds, rows, seq, n_batch):
    d = x_all.shape[1]
    tm = _tile(math.gcd(seq, rows), 256, 8)
    tpb = seq // tm
    k_sh, k_sc = kinds
    return pl.pallas_call(
        _norm_mod_kernel,
        out_shape=jax.ShapeDtypeStruct((rows, d), bf16),
        grid=(rows // tm,),
        in_specs=[
            pl.BlockSpec((tm, d), lambda i: (i, 0)),
            pl.BlockSpec((1, d), lambda i: (0, 0)),
            pl.BlockSpec((None, None, 1, d), lambda i: (k_sh, _group_of(i, tpb, n_batch), 0, 0)),
            pl.BlockSpec((None, None, 1, d), lambda i: (k_sc, _group_of(i, tpb, n_batch), 0, 0)),
        ],
        out_specs=pl.BlockSpec((tm, d), lambda i: (i, 0)),
        compiler_params=_params("parallel"),
        name="norm_mod",
    )(x_all, g_row, mods, mods)


def _route(logits_t, bias_col, carry, before):
    n_exp, tm = logits_t.shape
    gs = n_exp // ROUTE_GROUPS
    neg = -jnp.inf
    scores = _sigmoid(logits_t)
    choice = scores + bias_col

    iota_g = lax.broadcasted_iota(jnp.int32, (gs, tm), 0)
    grp_rows = []
    for g in range(ROUTE_GROUPS):
        xg = choice[g * gs:(g + 1) * gs, :]
        m1 = jnp.max(xg, axis=0, keepdims=True)
        i1 = jnp.min(jnp.where(xg == m1, iota_g, gs), axis=0, keepdims=True)
        m2 = jnp.max(jnp.where(iota_g == i1, neg, xg), axis=0, keepdims=True)
        grp_rows.append(m1 + m2)
    gsc = jnp.concatenate(grp_rows, axis=0)

    iota_gr = lax.broadcasted_iota(jnp.int32, (ROUTE_GROUPS, tm), 0)
    gsel = jnp.zeros((ROUTE_GROUPS, tm), f32)
    for _ in range(ROUTE_TOPK_GROUPS):
        mx = jnp.max(gsc, axis=0, keepdims=True)
        ix = jnp.min(jnp.where(gsc == mx, iota_gr, ROUTE_GROUPS), axis=0, keepdims=True)
        hit = iota_gr == ix
        gsel = jnp.where(hit, 1.0, gsel)
        gsc = jnp.where(hit, neg, gsc)
    emask = jnp.concatenate(
        [jnp.broadcast_to(gsel[g:g + 1, :], (gs, tm)) for g in range(ROUTE_GROUPS)], axis=0)

    masked = jnp.where(emask > 0, choice, neg)
    iota_e = lax.broadcasted_iota(jnp.int32, (n_exp, tm), 0)
    sel = jnp.zeros((n_exp, tm), f32)
    hits, e_rows, w_rows = [], [], []
    for _ in range(ROUTE_TOP_K):
        mx = jnp.max(masked, axis=0, keepdims=True)
        ix = jnp.min(jnp.where(masked == mx, iota_e, n_exp), axis=0, keepdims=True)
        hit = iota_e == ix
        sel = jnp.where(hit, 1.0, sel)
        masked = jnp.where(hit, neg, masked)
        hits.append(hit)
        e_rows.append(ix)
        w_rows.append(jnp.sum(jnp.where(hit, scores, 0.0), axis=0, keepdims=True))
    pos = _dot(sel.astype(bf16), before) + carry
    p_rows = [jnp.sum(jnp.where(hit, pos, 0.0), axis=0, keepdims=True) for hit in hits]
    w = jnp.concatenate(w_rows, axis=0)
    w = w / jnp.sum(w, axis=0, keepdims=True) * ROUTED_SCALE
    new_carry = carry + jnp.sum(sel, axis=1, keepdims=True)
    return (jnp.concatenate(e_rows, axis=0), jnp.concatenate(p_rows, axis=0).astype(jnp.int32), w,
            new_carry)


def _norm_mod_router_kernel(x_ref, g_ref, sh_ref, sc_ref, rwh_ref, rwl_ref, rb_ref, before_ref,
                            m_ref, slab_ref, e_ref, p_ref, w_ref, cnt_ref, carry_ref):
    @pl.when(pl.program_id(0) == 0)
    def _():
        carry_ref[...] = jnp.zeros_like(carry_ref)

    y = _norm_mod(x_ref[...], g_ref[...], sh_ref[...], sc_ref[...])
    m_ref[...] = y.astype(m_ref.dtype)
    _store_slabs(slab_ref, y, y.shape[1] // LANES)
    y_hi, y_lo = _split_bf16(y)
    rwh = rwh_ref[...]
    logits_t = _dot_nt(rwh, y_hi) + _dot_nt(rwh, y_lo) + _dot_nt(rwl_ref[...], y_hi)
    e_idx, rank, w, carry = _route(logits_t, rb_ref[...], carry_ref[...], before_ref[...])
    e_ref[...] = e_idx
    p_ref[...] = rank
    w_ref[...] = w
    carry_ref[...] = carry
    cnt_ref[...] = carry.astype(jnp.int32)


def _norm_mod_router_call(x_all, g_row, mods, kinds, rows, seq, n_batch, rw_t_hi, rw_t_lo, rb_col):
    d = x_all.shape[1]
    n_sub = d // LANES
    n_exp = rw_t_hi.shape[0]
    tm = _tile(math.gcd(seq, rows), 256, 128)
    tpb = seq // tm
    k_sh, k_sc = kinds
    before = jnp.asarray(np.arange(tm)[:, None] < np.arange(tm)[None, :], dtype=bf16)
    top = pl.BlockSpec((ROUTE_TOP_K, tm), lambda i: (0, i))
    return pl.pallas_call(
        _norm_mod_router_kernel,
        out_shape=(jax.ShapeDtypeStruct((rows, d), bf16),
                   jax.ShapeDtypeStruct((rows * n_sub, LANES), f32),
                   jax.ShapeDtypeStruct((ROUTE_TOP_K, rows), jnp.int32),
                   jax.ShapeDtypeStruct((ROUTE_TOP_K, rows), jnp.int32),
                   jax.ShapeDtypeStruct((ROUTE_TOP_K, rows), f32),
                   jax.ShapeDtypeStruct((n_exp, 1), jnp.int32)),
        grid=(rows // tm,),
        in_specs=[
            pl.BlockSpec((tm, d), lambda i: (i, 0)),
            pl.BlockSpec((1, d), lambda i: (0, 0)),
            pl.BlockSpec((None, None, 1, d), lambda i: (k_sh, _group_of(i, tpb, n_batch), 0, 0)),
            pl.BlockSpec((None, None, 1, d), lambda i: (k_sc, _group_of(i, tpb, n_batch), 0, 0)),
            pl.BlockSpec((n_exp, d), lambda i: (0, 0)),
            pl.BlockSpec((n_exp, d), lambda i: (0, 0)),
            pl.BlockSpec((n_exp, 1), lambda i: (0, 0)),
            pl.BlockSpec((tm, tm), lambda i: (0, 0)),
        ],
        out_specs=(pl.BlockSpec((tm, d), lambda i: (i, 0)),
                   pl.BlockSpec((tm * n_sub, LANES), lambda i: (i, 0)), top, top, top,
                   pl.BlockSpec((n_exp, 1), lambda i: (0, 0))),
        scratch_shapes=[pltpu.VMEM((n_exp, 1), f32)],
        compiler_params=_params("arbitrary"),
        name="norm_mod_router",
    )(x_all, g_row, mods, mods, rw_t_hi, rw_t_lo, rb_col, before)


def _mm_kernel(a_ref, w_ref, o_ref):
    o_ref[...] = _dot(a_ref[...], w_ref[...]).astype(o_ref.dtype)


def _mm_call(a, w, out_dtype, rows=None, tm_pref=512, tn_pref=1024):
    rows = a.shape[0] if rows is None else rows
    k = a.shape[1]
    n = w.shape[1]
    tm = _tile(rows, tm_pref, 8)
    tn = _tile(n, tn_pref, 128)
    return pl.pallas_call(
        _mm_kernel,
        out_shape=jax.ShapeDtypeStruct((rows, n), out_dtype),
        grid=(rows // tm, n // tn),
        in_specs=[pl.BlockSpec((tm, k), lambda i, j: (i, 0)), pl.BlockSpec((k, tn), lambda i, j: (0, j))],
        out_specs=pl.BlockSpec((tm, tn), lambda i, j: (i, j)),
        compiler_params=_params("parallel", "arbitrary"),
        name="matmul",
    )(a, w)


def _mm_resid_kernel(a_ref, w_ref, x_ref, gate_ref, o_ref):
    o_ref[...] = x_ref[...] + gate_ref[...] * _dot(a_ref[...], w_ref[...])


def _mm_resid_call(a, w, x_all, mods, kind, rows, seq, n_batch):
    k = a.shape[1]
    n = w.shape[1]
    tm = _tile(math.gcd(seq, rows), 512, 8)
    tn = _tile(n, 512, 128)
    tpb = seq // tm
    return pl.pallas_call(
        _mm_resid_kernel,
        out_shape=jax.ShapeDtypeStruct((rows, n), f32),
        grid=(rows // tm, n // tn),
        in_specs=[
            pl.BlockSpec((tm, k), lambda i, j: (i, 0)),
            pl.BlockSpec((k, tn), lambda i, j: (0, j)),
            pl.BlockSpec((tm, tn), lambda i, j: (i, j)),
            pl.BlockSpec((None, None, 1, tn), lambda i, j: (kind, _group_of(i, tpb, n_batch), 0, j)),
        ],
        out_specs=pl.BlockSpec((tm, tn), lambda i, j: (i, j)),
        compiler_params=_params("parallel", "arbitrary"),
        name="matmul_resid",
    )(a, w, x_all, mods)


def _dft_tables(n):
    idx = jnp.arange(n, dtype=jnp.int32)
    ang = ((idx[:, None] * idx[None, :]) % n).astype(f32) * (2.0 * np.pi / n)
    s = float(n) ** -0.5
    return (jnp.cos(ang) * s).astype(bf16), (jnp.sin(ang) * s).astype(bf16)


def _chan_dft_call(u, wcs, rows):
    d = u.shape[1]
    cg = d // FNET_GROUPS
    tm = _tile(rows, 512, 8)
    return pl.pallas_call(
        _mm_kernel,
        out_shape=jax.ShapeDtypeStruct((rows, 2 * d), bf16),
        grid=(rows // tm, FNET_GROUPS),
        in_specs=[pl.BlockSpec((tm, cg), lambda i, g: (i, g)), pl.BlockSpec((cg, 2 * cg), lambda i, g: (0, 0))],
        out_specs=pl.BlockSpec((tm, 2 * cg), lambda i, g: (i, g)),
        compiler_params=_params("parallel", "arbitrary"),
        name="fnet_chan_dft",
    )(u, wcs)


def _pos_dft_kernel(c_ref, s_ref, pq_ref, o_ref):
    cg = o_ref.shape[-1]
    acc = _dot(c_ref[...], pq_ref[:, :cg]) - _dot(s_ref[...], pq_ref[:, cg:])
    o_ref[...] = acc.astype(o_ref.dtype)


def _pos_dft_call(pq, cos_t, sin_t, seq, n_batch, first_block):
    d = pq.shape[1] // 2
    cg = d // FNET_GROUPS
    tm = _tile(seq, 512, 8)
    return pl.pallas_call(
        _pos_dft_kernel,
        out_shape=jax.ShapeDtypeStruct((n_batch * seq, d), bf16),
        grid=(n_batch, FNET_GROUPS, seq // tm),
        in_specs=[
            pl.BlockSpec((tm, seq), lambda b, g, i: (i, 0)),
            pl.BlockSpec((tm, seq), lambda b, g, i: (i, 0)),
            pl.BlockSpec((seq, 2 * cg), lambda b, g, i: (first_block + b, g)),
        ],
        out_specs=pl.BlockSpec((tm, cg), lambda b, g, i: (b * (seq // tm) + i, g)),
        compiler_params=_params("parallel", "parallel", "arbitrary"),
        name="fnet_pos_dft",
    )(cos_t, sin_t, pq)


def _glu(x, wg, wu, wd):
    hg = _dot(x, wg)
    hu = _dot(x, wu)
    return _dot(((hg * _sigmoid(hg)) * hu).astype(bf16), wd)


def _shared_kernel(m_ref, wg_ref, wu_ref, wd_ref, o_ref):
    j = pl.program_id(1)
    y = _glu(m_ref[...], wg_ref[...], wu_ref[...], wd_ref[...])

    @pl.when(j == 0)
    def _():
        o_ref[...] = y

    @pl.when(j > 0)
    def _():
        o_ref[...] += y


def _shared_call(m, wg, wu, wd, rows):
    d = m.shape[1]
    n_chunk, _, fdim = wg.shape
    tm = _tile(rows, 512, 8)
    return pl.pallas_call(
        _shared_kernel,
        out_shape=jax.ShapeDtypeStruct((rows, d), f32),
        grid=(rows // tm, n_chunk),
        in_specs=[
            pl.BlockSpec((tm, d), lambda i, j: (i, 0)),
            pl.BlockSpec((None, d, fdim), lambda i, j: (j, 0, 0)),
            pl.BlockSpec((None, d, fdim), lambda i, j: (j, 0, 0)),
            pl.BlockSpec((None, fdim, d), lambda i, j: (j, 0, 0)),
        ],
        out_specs=pl.BlockSpec((tm, d), lambda i, j: (i, 0)),
        compiler_params=_params("parallel", "arbitrary"),
        name="moe_shared",
    )(m, wg, wu, wd)


def _dispatch_plan(e_idx, rank, counts, rows, tile_rows):
    n_exp = counts.shape[0]
    n_tiles = (ROUTE_TOP_K * rows) // tile_rows + n_exp
    tiles_e = (counts + tile_rows - 1) // tile_rows
    tile_end = jnp.cumsum(tiles_e)
    first_slot = (tile_end - tiles_e) * tile_rows
    e_range = jnp.arange(n_exp, dtype=jnp.int32)
    dest = jnp.sum(jnp.where(e_idx[..., None] == e_range, first_slot, 0), axis=-1) + rank
    tile_ids = jnp.arange(n_tiles, dtype=jnp.int32)
    tile_expert = jnp.minimum(jnp.sum((tile_end[None, :] <= tile_ids[:, None]).astype(jnp.int32), axis=1),
                              n_exp - 1)
    tokens = jnp.broadcast_to(jnp.arange(rows, dtype=jnp.int32)[None, :], dest.shape)
    src = jnp.zeros(((n_tiles + 1) * tile_rows,), jnp.int32).at[dest.reshape(-1)].set(
        tokens.reshape(-1), unique_indices=True)
    return n_tiles, tile_expert, src.reshape(n_tiles + 1, 1, tile_rows), dest


def _slab_gather(idx_ref, n, n_sub, src_pitch, src_hbm, dst_buf, sem):
    for r in range(n):
        first = pl.multiple_of(idx_ref[0, r] * src_pitch, 8)
        pltpu.make_async_copy(src_hbm.at[pl.ds(first, n_sub), :],
                              dst_buf.at[pl.ds(r * SLAB_PITCH, n_sub), :], sem).start()


def _slab_gather_wait(n, n_sub, src_hbm, dst_buf, sem):
    pltpu.make_async_copy(src_hbm.at[pl.ds(0, n * n_sub), :], dst_buf.at[pl.ds(0, n * n_sub), :], sem).wait()


def _expert_kernel(te_ref, src_cur_ref, src_nxt_ref, m_hbm, wg_ref, wu_ref, wd_ref, ys_ref,
                   xbuf, sem, wgb, wub, wdb):
    t = pl.program_id(0)
    n_t = pl.num_programs(0)
    tile_rows = ys_ref.shape[0] // SLAB_PITCH
    n_sub = wgb.shape[0] // LANES
    slot = t % 2

    @pl.when(t == 0)
    def _():
        _slab_gather(src_cur_ref, tile_rows, n_sub, n_sub, m_hbm, xbuf.at[0], sem.at[0])

    @pl.when((t == 0) | (te_ref[t] != te_ref[jnp.maximum(t - 1, 0)]))
    def _():
        wgb[...] = wg_ref[...].astype(bf16)
        wub[...] = wu_ref[...].astype(bf16)
        wdb[...] = wd_ref[...].astype(bf16)

    _slab_gather(src_nxt_ref, tile_rows, n_sub, n_sub, m_hbm, xbuf.at[1 - slot], sem.at[1 - slot])
    _slab_gather_wait(tile_rows, n_sub, m_hbm, xbuf.at[slot], sem.at[slot])
    x = jnp.concatenate(
        [_load_slab_chunk(xbuf.at[slot], 0, tile_rows, s, SLAB_PITCH).astype(bf16) for s in range(n_sub)],
        axis=1)
    y = _glu(x, wgb[...], wub[...], wdb[...])
    _store_slabs(ys_ref, y, SLAB_PITCH)
    for s in range(n_sub, SLAB_PITCH):
        ys_ref[pl.ds(s, tile_rows, stride=SLAB_PITCH), :] = jnp.zeros((tile_rows, LANES), f32)

    @pl.when(t == n_t - 1)
    def _():
        _slab_gather_wait(tile_rows, n_sub, m_hbm, xbuf.at[1 - slot], sem.at[1 - slot])


def _expert_call(m, wg, wu, wd, layer, tile_expert, src, n_tiles, tile_rows):
    _, _, d, fdim = wg.shape
    smem_tile = lambda off: pl.BlockSpec((None, 1, tile_rows), lambda t, te: (t + off, 0, 0),
                                         memory_space=pltpu.SMEM)
    grid_spec = pltpu.PrefetchScalarGridSpec(
        num_scalar_prefetch=1,
        grid=(n_tiles,),
        in_specs=[
            smem_tile(0), smem_tile(1),
            pl.BlockSpec(memory_space=pl.ANY),
            pl.BlockSpec((None, None, d, fdim), lambda t, te: (layer, te[t], 0, 0)),
            pl.BlockSpec((None, None, d, fdim), lambda t, te: (layer, te[t], 0, 0)),
            pl.BlockSpec((None, None, fdim, d), lambda t, te: (layer, te[t], 0, 0)),
        ],
        out_specs=pl.BlockSpec((tile_rows * SLAB_PITCH, LANES), lambda t, te: (t, 0)),
        scratch_shapes=[
            pltpu.VMEM((2, tile_rows * SLAB_PITCH, LANES), f32),
            pltpu.SemaphoreType.DMA((2,)),
            pltpu.VMEM((d, fdim), bf16), pltpu.VMEM((d, fdim), bf16), pltpu.VMEM((fdim, d), bf16),
        ],
    )
    return pl.pallas_call(
        _expert_kernel,
        out_shape=jax.ShapeDtypeStruct((n_tiles * tile_rows * SLAB_PITCH, LANES), f32),
        grid_spec=grid_spec,
        compiler_params=_params("arbitrary"),
        name="moe_experts",
    )(tile_expert, src, src, m, wg, wu, wd)


def _combine_kernel(dst_cur_ref, dst_nxt_ref, ys_hbm, w_ref, x_ref, ysh_ref, gate_ref, *rest, final):
    if final:
        g_ref, o_ref, gbuf, sem = rest
    else:
        o_ref, gbuf, sem = rest
    i = pl.program_id(0)
    n_i = pl.num_programs(0)
    tm, d = o_ref.shape
    n_sub = d // LANES
    n_rows = ROUTE_TOP_K * tm
    slot = i % 2

    @pl.when(i == 0)
    def _():
        _slab_gather(dst_cur_ref, n_rows, n_sub, SLAB_PITCH, ys_hbm, gbuf.at[0], sem.at[0])

    _slab_gather(dst_nxt_ref, n_rows, n_sub, SLAB_PITCH, ys_hbm, gbuf.at[1 - slot], sem.at[1 - slot])
    _slab_gather_wait(n_rows, n_sub, ys_hbm, gbuf.at[slot], sem.at[slot])

    w = w_ref[...]
    wcols = [jnp.broadcast_to(w[:, k:k + 1], (tm, LANES)) for k in range(ROUTE_TOP_K)]
    chunks = []
    for s in range(n_sub):
        acc = ysh_ref[:, s * LANES:(s + 1) * LANES]
        for k in range(ROUTE_TOP_K):
            acc = acc + wcols[k] * _load_slab_chunk(gbuf.at[slot], k * tm * SLAB_PITCH, tm, s, SLAB_PITCH)
        chunks.append(acc)
    x = x_ref[...] + gate_ref[...] * jnp.concatenate(chunks, axis=1)
    if final:
        x = x * lax.rsqrt(jnp.mean(x * x, axis=-1, keepdims=True) + EPS) * g_ref[...]
    o_ref[...] = x

    @pl.when(i == n_i - 1)
    def _():
        _slab_gather_wait(n_rows, n_sub, ys_hbm, gbuf.at[1 - slot], sem.at[1 - slot])


def _combine_call(ys, dest, w_t, x_all, y_sh, mods, kind, rows, seq, n_batch, final_g_row=None):
    d = x_all.shape[1]
    tm = _tile(math.gcd(seq, rows), 64, 8)
    tpb = seq // tm
    n_i = rows // tm
    dst = dest.reshape(ROUTE_TOP_K, n_i, tm).transpose(1, 0, 2).reshape(n_i, 1, ROUTE_TOP_K * tm)
    dst = jnp.concatenate([dst, jnp.zeros((1, 1, ROUTE_TOP_K * tm), jnp.int32)], axis=0)
    smem_tile = lambda off: pl.BlockSpec((None, 1, ROUTE_TOP_K * tm), lambda i: (i + off, 0, 0),
                                         memory_space=pltpu.SMEM)
    in_specs = [
        smem_tile(0), smem_tile(1),
        pl.BlockSpec(memory_space=pl.ANY),
        pl.BlockSpec((tm, ROUTE_TOP_K), lambda i: (i, 0)),
        pl.BlockSpec((tm, d), lambda i: (i, 0)),
        pl.BlockSpec((tm, d), lambda i: (i, 0)),
        pl.BlockSpec((None, None, 1, d), lambda i: (kind, _group_of(i, tpb, n_batch), 0, 0)),
    ]
    args = [dst, dst, ys, w_t, x_all, y_sh, mods]
    if final_g_row is not None:
        in_specs.append(pl.BlockSpec((1, d), lambda i: (0, 0)))
        args.append(final_g_row)
    return pl.pallas_call(
        functools.partial(_combine_kernel, final=final_g_row is not None),
        out_shape=jax.ShapeDtypeStruct((rows, d), f32),
        grid=(n_i,),
        in_specs=in_specs,
        out_specs=pl.BlockSpec((tm, d), lambda i: (i, 0)),
        scratch_shapes=[pltpu.VMEM((2, ROUTE_TOP_K * tm * SLAB_PITCH, LANES), f32),
                        pltpu.SemaphoreType.DMA((2,))],
        compiler_params=_params("arbitrary"),
        name="moe_combine",
    )(*args)


def _gla_chunk(q, k, v, b, bx, st, reverse):
    ln, hd = q.shape
    c = GLA_BLOCK
    nb = ln // c

    def row(a, r):
        return a[r:r + 1, :]

    def bcast(rows):
        return jnp.concatenate([jnp.broadcast_to(r, (c, hd)) for r in rows], axis=0)

    beta = [row(bx, c * i + (c - 1 if reverse else 0)) for i in range(nb)]
    bend = [row(b, c * i + (0 if reverse else c - 1)) for i in range(nb)]
    btot = row(b, 0 if reverse else ln - 1)
    pos = (lambda i: nb - 1 - i) if reverse else (lambda i: i)
    blk_at = pos

    bstart = bcast(beta)
    e_q = jnp.exp(b - bstart)
    e_kd = jnp.exp(bstart - b)
    e_ke = jnp.exp(bcast(bend) - b)
    qe = q * e_q
    ke = k * e_ke

    lhs = [qe]
    for dist in range(2, nb):
        gap = []
        for i in range(nb):
            p = pos(i) - dist
            gap.append(jnp.exp(beta[i] - bend[blk_at(p)]) if p >= 0 else jnp.ones((1, hd), f32))
        lhs.append(qe * bcast(gap))
    p_off = _dot_nt(jnp.concatenate(lhs, axis=0).astype(bf16), ke.astype(bf16))
    p_diag = _dot_nt(qe.astype(bf16), (k * e_kd).astype(bf16))

    ti = lax.broadcasted_iota(jnp.int32, (ln, ln), 0)
    si = lax.broadcasted_iota(jnp.int32, (ln, ln), 1)
    tb, sb = ti // c, si // c
    dist_blk = (sb - tb) if reverse else (tb - sb)
    causal = (si >= ti) if reverse else (si <= ti)
    a = jnp.where((dist_blk == 0) & causal, p_diag, 0.0)
    for dist in range(1, nb):
        a = jnp.where(dist_blk == dist, p_off[(dist - 1) * ln:dist * ln, :], a)

    o = _dot(a.astype(bf16), v.astype(bf16))
    q_in = qe * jnp.exp(bstart)
    o = o + _dot_nt(q_in.astype(bf16), st.astype(bf16))
    khat = ke * bcast([jnp.exp(btot - bend[i]) for i in range(nb)])
    st_new = st * jnp.exp(btot) + _dot_tn(v.astype(bf16), khat.astype(bf16))
    return o, st_new


def _gla_kernel(q_ref, f_ref, v_ref, lb_ref, tri_ref, o_ref, st_ref, *, reverse):
    @pl.when(pl.program_id(2) == 0)
    def _():
        st_ref[...] = jnp.zeros_like(st_ref)

    rows, width = q_ref.shape
    hd = HEAD_DIM
    lb = lb_ref[...]
    f = lb + (1.0 - lb) * _sigmoid(f_ref[...].astype(f32))
    lf = jnp.log(f)
    lf_hi, lf_lo = _split_bf16(lf)
    tri = tri_ref[...]
    b = _dot(tri, lf_hi) + _dot(tri, lf_lo)
    bx = b - lf
    kk = 1.0 - f
    qq = q_ref[...].astype(f32) * (float(hd) ** -0.5)
    vv = v_ref[...].astype(f32)

    n_chunks = rows // GLA_CHUNK
    order = range(n_chunks - 1, -1, -1) if reverse else range(n_chunks)
    for h in range(width // hd):
        cs = slice(h * hd, (h + 1) * hd)
        st = st_ref[h]
        for ci in order:
            rs = slice(ci * GLA_CHUNK, (ci + 1) * GLA_CHUNK)
            o, st = _gla_chunk(qq[rs, cs], kk[rs, cs], vv[rs, cs], b[rs, cs], bx[rs, cs], st, reverse)
            o_ref[rs, cs] = o
        st_ref[h] = st


def _gla_call(proj, lb_row, tri, f_section, seq, ctx_len, n_batch, reverse):
    d = proj.shape[1] // 5
    r = GLA_STEP_ROWS
    w = GLA_HEADS_PER_STEP * HEAD_DIM
    nl, nc = seq // r, ctx_len // r
    ctx_base = n_batch * nl
    wb = d // w

    def row_block(bi, s):
        cs = jnp.minimum(s, nc - 1)
        ls = jnp.maximum(s - nc, 0)
        if reverse:
            cs, ls = nc - 1 - cs, nl - 1 - ls
        return jnp.where(s < nc, ctx_base + bi * nc + cs, bi * nl + ls)

    def out_block(bi, s):
        ls = jnp.maximum(s - nc, 0)
        if reverse:
            ls = nl - 1 - ls
        return bi * nl + ls

    def sec(k):
        return pl.BlockSpec((r, w), lambda bi, hb, s: (row_block(bi, s), k * wb + hb))

    return pl.pallas_call(
        functools.partial(_gla_kernel, reverse=reverse),
        out_shape=jax.ShapeDtypeStruct((n_batch * seq, d), f32),
        grid=(n_batch, wb, nc + nl),
        in_specs=[
            sec(0), sec(f_section), sec(3),
            pl.BlockSpec((1, w), lambda bi, hb, s: (0, hb)),
            pl.BlockSpec((r, r), lambda bi, hb, s: (0, 0)),
        ],
        out_specs=pl.BlockSpec((r, w), lambda bi, hb, s: (out_block(bi, s), hb)),
        scratch_shapes=[pltpu.VMEM((GLA_HEADS_PER_STEP, HEAD_DIM, HEAD_DIM), f32)],
        compiler_params=_params("parallel", "parallel", "arbitrary"),
        name="gla_scan_rev" if reverse else "gla_scan_fwd",
    )(proj, proj, proj, lb_row, tri)


def _chunk_tri(rows, reverse):
    t = np.arange(rows)[:, None]
    s = np.arange(rows)[None, :]
    same = (t // GLA_CHUNK) == (s // GLA_CHUNK)
    tri = same & ((s >= t) if reverse else (s <= t))
    return jnp.asarray(tri, dtype=bf16)


def _readout_kernel(of_ref, ob_ref, g_ref, gn_ref, o_ref):
    hd = HEAD_DIM
    gn = gn_ref[...]
    for h in range(o_ref.shape[-1] // hd):
        cs = slice(h * hd, (h + 1) * hd)
        o = of_ref[:, cs] + ob_ref[:, cs]
        g = g_ref[:, cs].astype(f32)
        y = o * lax.rsqrt(jnp.mean(o * o, axis=-1, keepdims=True) + EPS) * gn
        o_ref[:, cs] = (y * (g * _sigmoid(g))).astype(o_ref.dtype)


def _readout_call(o_f, o_b, proj, gn_row, rows):
    d = o_f.shape[1]
    tm = _tile(rows, 256, 8)
    w = _tile(d, 512, HEAD_DIM)
    wb = d // w
    return pl.pallas_call(
        _readout_kernel,
        out_shape=jax.ShapeDtypeStruct((rows, d), bf16),
        grid=(rows // tm, wb),
        in_specs=[
            pl.BlockSpec((tm, w), lambda i, j: (i, j)),
            pl.BlockSpec((tm, w), lambda i, j: (i, j)),
            pl.BlockSpec((tm, w), lambda i, j: (i, 4 * wb + j)),
            pl.BlockSpec((1, HEAD_DIM), lambda i, j: (0, 0)),
        ],
        out_specs=pl.BlockSpec((tm, w), lambda i, j: (i, j)),
        compiler_params=_params("parallel", "arbitrary"),
        name="hgrn_readout",
    )(o_f, o_b, proj, gn_row)


def kernel(x, c, ctx, c_ctx, ada_w, ada_b, norm1_g, norm2_g, fnet_w_in, fnet_w_out, hgrn_w_in, hgrn_lb,
           hgrn_gnorm_g, hgrn_w_out, router_w, router_b, exp_w_gate, exp_w_up, exp_w_down, sh_w_gate,
           sh_w_up, sh_w_down, final_g):
    n_batch, seq, d = x.shape
    ctx_len = ctx.shape[1]
    depth = ada_w.shape[0]
    assert depth == 2 and n_batch + 1 <= MOD_ROWS
    assert seq % GLA_STEP_ROWS == 0 and ctx_len % GLA_STEP_ROWS == 0 and seq % ctx_len == 0
    n_lat, n_ctx = n_batch * seq, n_batch * ctx_len
    n_all = n_lat + n_ctx
    fdim = exp_w_gate.shape[-1]
    n_sh = sh_w_gate.shape[-1] // fdim

    lb_all = jnp.cumsum(jax.nn.softmax(hgrn_lb.astype(f32), axis=0), axis=0)
    lb_all = lb_all - lb_all[0]

    x_all = jnp.concatenate([x.reshape(n_lat, d), ctx.reshape(n_ctx, d)], axis=0)
    cvec = jnp.concatenate([c, c_ctx[None, :], jnp.zeros((MOD_ROWS - n_batch - 1, d), f32)], axis=0)
    ada_b3 = ada_b[:, None, :]

    def channel_mixer(i, x_in, mods, rows, last):
        rw_hi, rw_lo = _split_bf16(router_w[i].T)
        m, m_slabs, e_idx, rank, w_k, counts = _norm_mod_router_call(
            x_in, norm2_g[i][None, :], mods, (3, 4), rows, seq, n_batch, rw_hi, rw_lo, router_b[i][:, None])
        n_tiles, tile_expert, src, dest = _dispatch_plan(e_idx, rank, counts[:, 0], rows, MOE_TILE_ROWS)
        ys = _expert_call(m_slabs, exp_w_gate, exp_w_up, exp_w_down, i, tile_expert, src, n_tiles,
                          MOE_TILE_ROWS)
        split_in = lambda ws: ws.astype(bf16).reshape(d, n_sh, fdim).transpose(1, 0, 2)
        y_sh = _shared_call(m, split_in(sh_w_gate[i]), split_in(sh_w_up[i]),
                            sh_w_down[i].astype(bf16).reshape(n_sh, fdim, d), rows)
        return _combine_call(ys, dest, w_k.T, x_in, y_sh, mods, 5, rows, seq, n_batch,
                             final_g[None, :] if last else None)

    mods = _ada_call(cvec, ada_w, ada_b3, 0)
    a = _norm_mod_call(x_all, norm1_g[0][None, :], mods, (0, 1), n_all, seq, n_batch)
    u = _mm_call(a, fnet_w_in[0].astype(bf16), bf16)
    cg = d // FNET_GROUPS
    cc, sc = _dft_tables(cg)
    pq = _chan_dft_call(u, jnp.concatenate([cc, sc], axis=1), n_all)
    z = jnp.concatenate([
        _pos_dft_call(pq, *_dft_tables(seq), seq, n_batch, 0),
        _pos_dft_call(pq, *_dft_tables(ctx_len), ctx_len, n_batch, n_lat // ctx_len),
    ], axis=0)
    x_all = _mm_resid_call(z, fnet_w_out[0].astype(bf16), x_all, mods, 2, n_all, seq, n_batch)
    x_all = channel_mixer(0, x_all, mods, n_all, False)

    mods = _ada_call(cvec, ada_w, ada_b3, 1)
    a = _norm_mod_call(x_all, norm1_g[1][None, :], mods, (0, 1), n_all, seq, n_batch)
    proj = _mm_call(a, hgrn_w_in[0].astype(bf16), bf16)
    o_f = _gla_call(proj, lb_all[1, 0][None, :], _chunk_tri(GLA_STEP_ROWS, False), 1, seq, ctx_len, n_batch, False)
    o_b = _gla_call(proj, lb_all[1, 1][None, :], _chunk_tri(GLA_STEP_ROWS, True), 2, seq, ctx_len, n_batch, True)
    r = _readout_call(o_f, o_b, proj, hgrn_gnorm_g[0][None, :], n_lat)
    x_lat = _mm_resid_call(r, hgrn_w_out[0].astype(bf16), x_all, mods, 2, n_lat, seq, n_batch)
    out = channel_mixer(1, x_lat, mods, n_lat, True)
    return out.reshape(n_batch, seq, d)
```

```python
import functools
import math

import jax
import jax.numpy as jnp
import numpy as np
from jax import lax
from jax.experimental import pallas as pl
from jax.experimental.pallas import tpu as pltpu

EPS = 1e-6
FNET_GROUPS = 8
HEAD_DIM = 128
ROUTE_GROUPS = 8
ROUTE_TOPK_GROUPS = 4
ROUTE_TOP_K = 8
ROUTED_SCALE = 2.5
GLA_CHUNK = 64
GLA_BLOCK = 16
GLA_STEP_ROWS = 256
GLA_HEADS_PER_STEP = 2
MOD_ROWS = 8
MOE_TILE_ROWS = 256
LANES = 128
SLAB_PITCH = 40

V7X_VMEM_LIMIT_BYTES = 56 * 1024 * 1024

f32 = jnp.float32
bf16 = jnp.bfloat16


def _tile(n, pref, mult):
    if n <= pref:
        return n
    t = (pref // mult) * mult
    while t > 0 and n % t:
        t -= mult
    assert t > 0, (n, pref, mult)
    return t


def _params(*sem):
    return pltpu.CompilerParams(dimension_semantics=sem, vmem_limit_bytes=V7X_VMEM_LIMIT_BYTES)


def _sigmoid(x):
    return 1.0 / (1.0 + jnp.exp(-x))


def _dot(a, b):
    return jnp.dot(a, b, preferred_element_type=f32)


def _dot_nt(a, b):
    return lax.dot_general(a, b, (((1,), (1,)), ((), ())), preferred_element_type=f32)


def _dot_tn(a, b):
    return lax.dot_general(a, b, (((0,), (0,)), ((), ())), preferred_element_type=f32)


def _split_bf16(x):
    hi = x.astype(bf16)
    lo = (x - hi.astype(f32)).astype(bf16)
    return hi, lo


def _store_slabs(ref, rows_val, pitch):
    n = rows_val.shape[0]
    for s in range(rows_val.shape[1] // LANES):
        ref[pl.ds(s, n, stride=pitch), :] = rows_val[:, s * LANES:(s + 1) * LANES]


def _load_slab_chunk(ref, first_row, n, s, pitch):
    return ref[pl.ds(first_row + s, n, stride=pitch), :]


def _ada_kernel(c_ref, w_ref, b_ref, o_ref):
    c = c_ref[...]
    s_hi, s_lo = _split_bf16(c * _sigmoid(c))
    w = w_ref[...].astype(bf16)
    o_ref[...] = _dot(s_hi, w) + _dot(s_lo, w) + b_ref[...]


def _ada_call(cvec, ada_w, ada_b3, layer):
    rows, d = cvec.shape
    n6 = ada_w.shape[2]
    tn = _tile(n6, 512, 128)
    out = pl.pallas_call(
        _ada_kernel,
        out_shape=jax.ShapeDtypeStruct((rows, n6), f32),
        grid=(n6 // tn,),
        in_specs=[
            pl.BlockSpec((rows, d), lambda j: (0, 0)),
            pl.BlockSpec((None, d, tn), lambda j: (layer, 0, j)),
            pl.BlockSpec((None, 1, tn), lambda j: (layer, 0, j)),
        ],
        out_specs=pl.BlockSpec((rows, tn), lambda j: (0, j)),
        compiler_params=_params("arbitrary"),
        name="ada_mod",
    )(cvec, ada_w, ada_b3)
    return out.reshape(rows, 6, d).transpose(1, 0, 2)[:, :, None, :]


def _norm_mod(x, g, shift, scale):
    ms = jnp.mean(x * x, axis=-1, keepdims=True)
    y = x * lax.rsqrt(ms + EPS) * g
    return y * (1.0 + scale) + shift


def _norm_mod_kernel(x_ref, g_ref, sh_ref, sc_ref, o_ref):
    o_ref[...] = _norm_mod(x_ref[...], g_ref[...], sh_ref[...], sc_ref[...]).astype(o_ref.dtype)


def _group_of(i, tiles_per_batch, n_batch):
    return jnp.minimum(i // tiles_per_batch, n_batch)


def _norm_mod_call(x_all, g_row, mods, kinds, rows, seq, n_batch):
    d = x_all.shape[1]
    tm = _tile(math.gcd(seq, rows), 256, 8)
    tpb = seq // tm
    k_sh, k_sc = kinds
    return pl.pallas_call(
        _norm_mod_kernel,
        out_shape=jax.ShapeDtypeStruct((rows, d), bf16),
        grid=(rows // tm,),
        in_specs=[
            pl.BlockSpec((tm, d), lambda i: (i, 0)),
            pl.BlockSpec((1, d), lambda i: (0, 0)),
            pl.BlockSpec((None, None, 1, d), lambda i: (k_sh, _group_of(i, tpb, n_batch), 0, 0)),
            pl.BlockSpec((None, None, 1, d), lambda i: (k_sc, _group_of(i, tpb, n_batch), 0, 0)),
        ],
        out_specs=pl.BlockSpec((tm, d), lambda i: (i, 0)),
        compiler_params=_params("parallel"),
        name="norm_mod",
    )(x_all, g_row, mods, mods)


def _route(logits_t, bias_col, carry, before):
    n_exp, tm = logits_t.shape
    gs = n_exp // ROUTE_GROUPS
    neg = -jnp.inf
    scores = _sigmoid(logits_t)
    choice = scores + bias_col

    iota_g = lax.broadcasted_iota(jnp.int32, (gs, tm), 0)
    grp_rows = []
    for g in range(ROUTE_GROUPS):
        xg = choice[g * gs:(g + 1) * gs, :]
        m1 = jnp.max(xg, axis=0, keepdims=True)
        i1 = jnp.min(jnp.where(xg == m1, iota_g, gs), axis=0, keepdims=True)
        m2 = jnp.max(jnp.where(iota_g == i1, neg, xg), axis=0, keepdims=True)
        grp_rows.append(m1 + m2)
    gsc = jnp.concatenate(grp_rows, axis=0)

    iota_gr = lax.broadcasted_iota(jnp.int32, (ROUTE_GROUPS, tm), 0)
    gsel = jnp.zeros((ROUTE_GROUPS, tm), f32)
    for _ in range(ROUTE_TOPK_GROUPS):
        mx = jnp.max(gsc, axis=0, keepdims=True)
        ix = jnp.min(jnp.where(gsc == mx, iota_gr, ROUTE_GROUPS), axis=0, keepdims=True)
        hit = iota_gr == ix
        gsel = jnp.where(hit, 1.0, gsel)
        gsc = jnp.where(hit, neg, gsc)
    emask = jnp.concatenate(
        [jnp.broadcast_to(gsel[g:g + 1, :], (gs, tm)) for g in range(ROUTE_GROUPS)], axis=0)

    masked = jnp.where(emask > 0, choice, neg)
    iota_e = lax.broadcasted_iota(jnp.int32, (n_exp, tm), 0)
    sel = jnp.zeros((n_exp, tm), f32)
    hits, e_rows, w_rows = [], [], []
    for _ in range(ROUTE_TOP_K):
        mx = jnp.max(masked, axis=0, keepdims=True)
        ix = jnp.min(jnp.where(masked == mx, iota_e, n_exp), axis=0, keepdims=True)
        hit = iota_e == ix
        sel = jnp.where(hit, 1.0, sel)
        masked = jnp.where(hit, neg, masked)
        hits.append(hit)
        e_rows.append(ix)
        w_rows.append(jnp.sum(jnp.where(hit, scores, 0.0), axis=0, keepdims=True))
    pos = _dot(sel.astype(bf16), before) + carry
    p_rows = [jnp.sum(jnp.where(hit, pos, 0.0), axis=0, keepdims=True) for hit in hits]
    w = jnp.concatenate(w_rows, axis=0)
    w = w / jnp.sum(w, axis=0, keepdims=True) * ROUTED_SCALE
    new_carry = carry + jnp.sum(sel, axis=1, keepdims=True)
    return (jnp.concatenate(e_rows, axis=0), jnp.concatenate(p_rows, axis=0).astype(jnp.int32), w,
            new_carry)


def _norm_mod_router_kernel(x_ref, g_ref, sh_ref, sc_ref, rwh_ref, rwl_ref, rb_ref, before_ref,
                            m_ref, slab_ref, e_ref, p_ref, w_ref, cnt_ref, carry_ref):
    @pl.when(pl.program_id(0) == 0)
    def _():
        carry_ref[...] = jnp.zeros_like(carry_ref)

    y = _norm_mod(x_ref[...], g_ref[...], sh_ref[...], sc_ref[...])
    m_ref[...] = y.astype(m_ref.dtype)
    _store_slabs(slab_ref, y, y.shape[1] // LANES)
    y_hi, y_lo = _split_bf16(y)
    rwh = rwh_ref[...]
    logits_t = _dot_nt(rwh, y_hi) + _dot_nt(rwh, y_lo) + _dot_nt(rwl_ref[...], y_hi)
    e_idx, rank, w, carry = _route(logits_t, rb_ref[...], carry_ref[...], before_ref[...])
    e_ref[...] = e_idx
    p_ref[...] = rank
    w_ref[...] = w
    carry_ref[...] = carry
    cnt_ref[...] = carry.astype(jnp.int32)


def _norm_mod_router_call(x_all, g_row, mods, kinds, rows, seq, n_batch, rw_t_hi, rw_t_lo, rb_col):
    d = x_all.shape[1]
    n_sub = d // LANES
    n_exp = rw_t_hi.shape[0]
    tm = _tile(math.gcd(seq, rows), 256, 128)
    tpb = seq // tm
    k_sh, k_sc = kinds
    before = jnp.asarray(np.arange(tm)[:, None] < np.arange(tm)[None, :], dtype=bf16)
    top = pl.BlockSpec((ROUTE_TOP_K, tm), lambda i: (0, i))
    return pl.pallas_call(
        _norm_mod_router_kernel,
        out_shape=(jax.ShapeDtypeStruct((rows, d), bf16),
                   jax.ShapeDtypeStruct((rows * n_sub, LANES), f32),
                   jax.ShapeDtypeStruct((ROUTE_TOP_K, rows), jnp.int32),
                   jax.ShapeDtypeStruct((ROUTE_TOP_K, rows), jnp.int32),
                   jax.ShapeDtypeStruct((ROUTE_TOP_K, rows), f32),
                   jax.ShapeDtypeStruct((n_exp, 1), jnp.int32)),
        grid=(rows // tm,),
        in_specs=[
            pl.BlockSpec((tm, d), lambda i: (i, 0)),
            pl.BlockSpec((1, d), lambda i: (0, 0)),
            pl.BlockSpec((None, None, 1, d), lambda i: (k_sh, _group_of(i, tpb, n_batch), 0, 0)),
            pl.BlockSpec((None, None, 1, d), lambda i: (k_sc, _group_of(i, tpb, n_batch), 0, 0)),
            pl.BlockSpec((n_exp, d), lambda i: (0, 0)),
            pl.BlockSpec((n_exp, d), lambda i: (0, 0)),
            pl.BlockSpec((n_exp, 1), lambda i: (0, 0)),
            pl.BlockSpec((tm, tm), lambda i: (0, 0)),
        ],
        out_specs=(pl.BlockSpec((tm, d), lambda i: (i, 0)),
                   pl.BlockSpec((tm * n_sub, LANES), lambda i: (i, 0)), top, top, top,
                   pl.BlockSpec((n_exp, 1), lambda i: (0, 0))),
        scratch_shapes=[pltpu.VMEM((n_exp, 1), f32)],
        compiler_params=_params("arbitrary"),
        name="norm_mod_router",
    )(x_all, g_row, mods, mods, rw_t_hi, rw_t_lo, rb_col, before)


def _mm_kernel(a_ref, w_ref, o_ref):
    o_ref[...] = _dot(a_ref[...], w_ref[...]).astype(o_ref.dtype)


def _cast_weight_once(w_ref, wb_ref):
    @pl.when(pl.program_id(1) == 0)
    def _():
        wb_ref[...] = w_ref[...].astype(bf16)


def _mm_w32_kernel(a_ref, w_ref, o_ref, wb_ref):
    _cast_weight_once(w_ref, wb_ref)
    o_ref[...] = _dot(a_ref[...], wb_ref[...]).astype(o_ref.dtype)


def _mm_call(a, w, out_dtype, rows=None):
    rows = a.shape[0] if rows is None else rows
    k = a.shape[1]
    n = w.shape[1]
    tm = _tile(rows, 512, 8)
    tn = _tile(n, 512, 128)
    return pl.pallas_call(
        _mm_w32_kernel,
        out_shape=jax.ShapeDtypeStruct((rows, n), out_dtype),
        grid=(n // tn, rows // tm),
        in_specs=[pl.BlockSpec((tm, k), lambda j, i: (i, 0)), pl.BlockSpec((k, tn), lambda j, i: (0, j))],
        out_specs=pl.BlockSpec((tm, tn), lambda j, i: (i, j)),
        scratch_shapes=[pltpu.VMEM((k, tn), bf16)],
        compiler_params=_params("arbitrary", "arbitrary"),
        name="matmul",
    )(a, w)


def _mm_resid_w32_kernel(a_ref, w_ref, x_ref, gate_ref, o_ref, wb_ref):
    _cast_weight_once(w_ref, wb_ref)
    o_ref[...] = x_ref[...] + gate_ref[...] * _dot(a_ref[...], wb_ref[...])


def _mm_resid_call(a, w, x_all, mods, kind, rows, seq, n_batch):
    k = a.shape[1]
    n = w.shape[1]
    tm = _tile(math.gcd(seq, rows), 512, 8)
    tn = _tile(n, 512, 128)
    tpb = seq // tm
    return pl.pallas_call(
        _mm_resid_w32_kernel,
        out_shape=jax.ShapeDtypeStruct((rows, n), f32),
        grid=(n // tn, rows // tm),
        in_specs=[
            pl.BlockSpec((tm, k), lambda j, i: (i, 0)),
            pl.BlockSpec((k, tn), lambda j, i: (0, j)),
            pl.BlockSpec((tm, tn), lambda j, i: (i, j)),
            pl.BlockSpec((None, None, 1, tn), lambda j, i: (kind, _group_of(i, tpb, n_batch), 0, j)),
        ],
        out_specs=pl.BlockSpec((tm, tn), lambda j, i: (i, j)),
        scratch_shapes=[pltpu.VMEM((k, tn), bf16)],
        compiler_params=_params("arbitrary", "arbitrary"),
        name="matmul_resid",
    )(a, w, x_all, mods)


def _dft_tables(n):
    r = 1 << (int(n).bit_length() // 2) if n & (n - 1) == 0 else 1
    col = jnp.arange(n, dtype=jnp.int32)[None, :]

    def rows_cs(row_ids):
        ang = ((row_ids[:, None] * col) % n).astype(f32) * (2.0 * np.pi / n)
        return jnp.cos(ang), jnp.sin(ang)

    ca, sa = rows_cs(jnp.arange(n // r, dtype=jnp.int32) * r)
    cb, sb = rows_cs(jnp.arange(r, dtype=jnp.int32))
    s = float(n) ** -0.5
    cos_t = (ca[:, None, :] * cb[None, :, :] - sa[:, None, :] * sb[None, :, :]).reshape(n, n)
    sin_t = (sa[:, None, :] * cb[None, :, :] + ca[:, None, :] * sb[None, :, :]).reshape(n, n)
    return (cos_t * s).astype(bf16), (sin_t * s).astype(bf16)


def _chan_dft_call(u, wcs, rows):
    d = u.shape[1]
    cg = d // FNET_GROUPS
    tm = _tile(rows, 512, 8)
    return pl.pallas_call(
        _mm_kernel,
        out_shape=jax.ShapeDtypeStruct((rows, 2 * d), bf16),
        grid=(rows // tm, FNET_GROUPS),
        in_specs=[pl.BlockSpec((tm, cg), lambda i, g: (i, g)), pl.BlockSpec((cg, 2 * cg), lambda i, g: (0, 0))],
        out_specs=pl.BlockSpec((tm, 2 * cg), lambda i, g: (i, g)),
        compiler_params=_params("parallel", "arbitrary"),
        name="fnet_chan_dft",
    )(u, wcs)


def _pos_dft_kernel(c_ref, s_ref, pq_ref, o_ref):
    cg = o_ref.shape[-1]
    acc = _dot(c_ref[...], pq_ref[:, :cg]) - _dot(s_ref[...], pq_ref[:, cg:])
    o_ref[...] = acc.astype(o_ref.dtype)


def _pos_dft_call(pq, cos_t, sin_t, seq, n_batch, first_block):
    d = pq.shape[1] // 2
    cg = d // FNET_GROUPS
    tm = _tile(seq, 512, 8)
    return pl.pallas_call(
        _pos_dft_kernel,
        out_shape=jax.ShapeDtypeStruct((n_batch * seq, d), bf16),
        grid=(n_batch, FNET_GROUPS, seq // tm),
        in_specs=[
            pl.BlockSpec((tm, seq), lambda b, g, i: (i, 0)),
            pl.BlockSpec((tm, seq), lambda b, g, i: (i, 0)),
            pl.BlockSpec((seq, 2 * cg), lambda b, g, i: (first_block + b, g)),
        ],
        out_specs=pl.BlockSpec((tm, cg), lambda b, g, i: (b * (seq // tm) + i, g)),
        compiler_params=_params("parallel", "parallel", "arbitrary"),
        name="fnet_pos_dft",
    )(cos_t, sin_t, pq)


def _glu(x, wg, wu, wd):
    hg = _dot(x, wg)
    hu = _dot(x, wu)
    return _dot(((hg * _sigmoid(hg)) * hu).astype(bf16), wd)


def _shared_kernel(m_ref, wg_ref, wu_ref, wd_ref, o_ref):
    j = pl.program_id(1)
    y = _glu(m_ref[...], wg_ref[...], wu_ref[...], wd_ref[...])

    @pl.when(j == 0)
    def _():
        o_ref[...] = y

    @pl.when(j > 0)
    def _():
        o_ref[...] += y


def _shared_call(m, wg, wu, wd, rows):
    d = m.shape[1]
    n_chunk, _, fdim = wg.shape
    tm = _tile(rows, 512, 8)
    return pl.pallas_call(
        _shared_kernel,
        out_shape=jax.ShapeDtypeStruct((rows, d), f32),
        grid=(rows // tm, n_chunk),
        in_specs=[
            pl.BlockSpec((tm, d), lambda i, j: (i, 0)),
            pl.BlockSpec((None, d, fdim), lambda i, j: (j, 0, 0)),
            pl.BlockSpec((None, d, fdim), lambda i, j: (j, 0, 0)),
            pl.BlockSpec((None, fdim, d), lambda i, j: (j, 0, 0)),
        ],
        out_specs=pl.BlockSpec((tm, d), lambda i, j: (i, 0)),
        compiler_params=_params("parallel", "arbitrary"),
        name="moe_shared",
    )(m, wg, wu, wd)


def _dispatch_plan(e_idx, rank, counts, rows, tile_rows):
    n_exp = counts.shape[0]
    n_tiles = (ROUTE_TOP_K * rows) // tile_rows + n_exp
    tiles_e = (counts + tile_rows - 1) // tile_rows
    tile_end = jnp.cumsum(tiles_e)
    first_slot = (tile_end - tiles_e) * tile_rows
    e_range = jnp.arange(n_exp, dtype=jnp.int32)
    dest = jnp.sum(jnp.where(e_idx[..., None] == e_range, first_slot, 0), axis=-1) + rank
    tile_ids = jnp.arange(n_tiles, dtype=jnp.int32)
    tile_expert = jnp.minimum(jnp.sum((tile_end[None, :] <= tile_ids[:, None]).astype(jnp.int32), axis=1),
                              n_exp - 1)
    tokens = jnp.broadcast_to(jnp.arange(rows, dtype=jnp.int32)[None, :], dest.shape)
    src = jnp.zeros(((n_tiles + 1) * tile_rows,), jnp.int32).at[dest.reshape(-1)].set(
        tokens.reshape(-1), unique_indices=True)
    return n_tiles, tile_expert, src.reshape(n_tiles + 1, 1, tile_rows), dest


def _slab_gather(idx_ref, n, n_sub, src_pitch, src_hbm, dst_buf, sem):
    for r in range(n):
        first = pl.multiple_of(idx_ref[0, r] * src_pitch, 8)
        pltpu.make_async_copy(src_hbm.at[pl.ds(first, n_sub), :],
                              dst_buf.at[pl.ds(r * SLAB_PITCH, n_sub), :], sem).start(priority=r % 2)


def _slab_gather_wait(n, n_sub, src_hbm, dst_buf, sem):
    pltpu.make_async_copy(src_hbm.at[pl.ds(0, n * n_sub), :], dst_buf.at[pl.ds(0, n * n_sub), :], sem).wait()


def _expert_kernel(te_ref, src_cur_ref, src_nxt_ref, m_hbm, wg_ref, wu_ref, wd_ref, ys_ref,
                   xbuf, sem, wgb, wub, wdb):
    t = pl.program_id(0)
    n_t = pl.num_programs(0)
    tile_rows = ys_ref.shape[0] // SLAB_PITCH
    n_sub = wgb.shape[0] // LANES
    slot = t % 2

    @pl.when(t == 0)
    def _():
        _slab_gather(src_cur_ref, tile_rows, n_sub, n_sub, m_hbm, xbuf.at[0], sem.at[0])

    @pl.when((t == 0) | (te_ref[t] != te_ref[jnp.maximum(t - 1, 0)]))
    def _():
        wgb[...] = wg_ref[...].astype(bf16)
        wub[...] = wu_ref[...].astype(bf16)
        wdb[...] = wd_ref[...].astype(bf16)

    _slab_gather(src_nxt_ref, tile_rows, n_sub, n_sub, m_hbm, xbuf.at[1 - slot], sem.at[1 - slot])
    _slab_gather_wait(tile_rows, n_sub, m_hbm, xbuf.at[slot], sem.at[slot])
    x = jnp.concatenate(
        [_load_slab_chunk(xbuf.at[slot], 0, tile_rows, s, SLAB_PITCH).astype(bf16) for s in range(n_sub)],
        axis=1)
    _store_slabs(ys_ref, _glu(x, wgb[...], wub[...], wdb[...]), SLAB_PITCH)
    for s in range(n_sub, SLAB_PITCH):
        ys_ref[pl.ds(s, tile_rows, stride=SLAB_PITCH), :] = jnp.zeros((tile_rows, LANES), f32)

    @pl.when(t == n_t - 1)
    def _():
        _slab_gather_wait(tile_rows, n_sub, m_hbm, xbuf.at[1 - slot], sem.at[1 - slot])


def _expert_call(m, wg, wu, wd, layer, tile_expert, src, n_tiles, tile_rows):
    _, _, d, fdim = wg.shape
    smem_tile = lambda off: pl.BlockSpec((None, 1, tile_rows), lambda t, te: (t + off, 0, 0),
                                         memory_space=pltpu.SMEM)
    grid_spec = pltpu.PrefetchScalarGridSpec(
        num_scalar_prefetch=1,
        grid=(n_tiles,),
        in_specs=[
            smem_tile(0), smem_tile(1),
            pl.BlockSpec(memory_space=pl.ANY),
            pl.BlockSpec((None, None, d, fdim), lambda t, te: (layer, te[t], 0, 0)),
            pl.BlockSpec((None, None, d, fdim), lambda t, te: (layer, te[t], 0, 0)),
            pl.BlockSpec((None, None, fdim, d), lambda t, te: (layer, te[t], 0, 0)),
        ],
        out_specs=pl.BlockSpec((tile_rows * SLAB_PITCH, LANES), lambda t, te: (t, 0)),
        scratch_shapes=[
            pltpu.VMEM((2, tile_rows * SLAB_PITCH, LANES), f32),
            pltpu.SemaphoreType.DMA((2,)),
            pltpu.VMEM((d, fdim), bf16), pltpu.VMEM((d, fdim), bf16), pltpu.VMEM((fdim, d), bf16),
        ],
    )
    return pl.pallas_call(
        _expert_kernel,
        out_shape=jax.ShapeDtypeStruct((n_tiles * tile_rows * SLAB_PITCH, LANES), f32),
        grid_spec=grid_spec,
        compiler_params=_params("arbitrary"),
        name="moe_experts",
    )(tile_expert, src, src, m, wg, wu, wd)


def _combine_kernel(dst_cur_ref, dst_nxt_ref, ys_hbm, w_ref, x_ref, ysh_ref, gate_ref, *rest, final):
    if final:
        g_ref, o_ref, gbuf, sem = rest
    else:
        o_ref, gbuf, sem = rest
    i = pl.program_id(0)
    n_i = pl.num_programs(0)
    tm, d = o_ref.shape
    n_sub = d // LANES
    n_rows = ROUTE_TOP_K * tm
    slot = i % 2

    @pl.when(i == 0)
    def _():
        _slab_gather(dst_cur_ref, n_rows, n_sub, SLAB_PITCH, ys_hbm, gbuf.at[0], sem.at[0])

    _slab_gather(dst_nxt_ref, n_rows, n_sub, SLAB_PITCH, ys_hbm, gbuf.at[1 - slot], sem.at[1 - slot])
    _slab_gather_wait(n_rows, n_sub, ys_hbm, gbuf.at[slot], sem.at[slot])

    w = w_ref[...]
    wcols = [jnp.broadcast_to(w[:, k:k + 1], (tm, LANES)) for k in range(ROUTE_TOP_K)]
    chunks = []
    for s in range(n_sub):
        acc = ysh_ref[:, s * LANES:(s + 1) * LANES]
        for k in range(ROUTE_TOP_K):
            acc = acc + wcols[k] * _load_slab_chunk(gbuf.at[slot], k * tm * SLAB_PITCH, tm, s, SLAB_PITCH)
        chunks.append(acc)
    x = x_ref[...] + gate_ref[...] * jnp.concatenate(chunks, axis=1)
    if final:
        x = x * lax.rsqrt(jnp.mean(x * x, axis=-1, keepdims=True) + EPS) * g_ref[...]
    o_ref[...] = x

    @pl.when(i == n_i - 1)
    def _():
        _slab_gather_wait(n_rows, n_sub, ys_hbm, gbuf.at[1 - slot], sem.at[1 - slot])


def _combine_call(ys, dest, w_t, x_all, y_sh, mods, kind, rows, seq, n_batch, final_g_row=None):
    d = x_all.shape[1]
    tm = _tile(math.gcd(seq, rows), 64, 8)
    tpb = seq // tm
    n_i = rows // tm
    dst = dest.reshape(ROUTE_TOP_K, n_i, tm).transpose(1, 0, 2).reshape(n_i, 1, ROUTE_TOP_K * tm)
    dst = jnp.concatenate([dst, jnp.zeros((1, 1, ROUTE_TOP_K * tm), jnp.int32)], axis=0)
    smem_tile = lambda off: pl.BlockSpec((None, 1, ROUTE_TOP_K * tm), lambda i: (i + off, 0, 0),
                                         memory_space=pltpu.SMEM)
    in_specs = [
        smem_tile(0), smem_tile(1),
        pl.BlockSpec(memory_space=pl.ANY),
        pl.BlockSpec((tm, ROUTE_TOP_K), lambda i: (i, 0)),
        pl.BlockSpec((tm, d), lambda i: (i, 0)),
        pl.BlockSpec((tm, d), lambda i: (i, 0)),
        pl.BlockSpec((None, None, 1, d), lambda i: (kind, _group_of(i, tpb, n_batch), 0, 0)),
    ]
    args = [dst, dst, ys, w_t, x_all, y_sh, mods]
    if final_g_row is not None:
        in_specs.append(pl.BlockSpec((1, d), lambda i: (0, 0)))
        args.append(final_g_row)
    return pl.pallas_call(
        functools.partial(_combine_kernel, final=final_g_row is not None),
        out_shape=jax.ShapeDtypeStruct((rows, d), f32),
        grid=(n_i,),
        in_specs=in_specs,
        out_specs=pl.BlockSpec((tm, d), lambda i: (i, 0)),
        scratch_shapes=[pltpu.VMEM((2, ROUTE_TOP_K * tm * SLAB_PITCH, LANES), f32),
                        pltpu.SemaphoreType.DMA((2,))],
        compiler_params=_params("arbitrary"),
        name="moe_combine",
    )(*args)


def _gla_chunk(q, k, v, b, bx, st, reverse):
    ln, hd = q.shape
    c = GLA_BLOCK
    nb = ln // c

    def row(a, r):
        return a[r:r + 1, :]

    def bcast(rows):
        return jnp.concatenate([jnp.broadcast_to(r, (c, hd)) for r in rows], axis=0)

    beta = [row(bx, c * i + (c - 1 if reverse else 0)) for i in range(nb)]
    bend = [row(b, c * i + (0 if reverse else c - 1)) for i in range(nb)]
    btot = row(b, 0 if reverse else ln - 1)
    pos = (lambda i: nb - 1 - i) if reverse else (lambda i: i)
    blk_at = pos

    bstart = bcast(beta)
    e_q = jnp.exp2(b - bstart)
    e_kd = jnp.exp2(bstart - b)
    e_ke = jnp.exp2(bcast(bend) - b)
    qe = q * e_q
    ke = k * e_ke

    lhs = [qe]
    for dist in range(2, nb):
        gap = []
        for i in range(nb):
            p = pos(i) - dist
            gap.append(jnp.exp2(beta[i] - bend[blk_at(p)]) if p >= 0 else jnp.ones((1, hd), f32))
        lhs.append(qe * bcast(gap))
    p_off = _dot_nt(jnp.concatenate(lhs, axis=0).astype(bf16), ke.astype(bf16))
    p_diag = _dot_nt(qe.astype(bf16), (k * e_kd).astype(bf16))

    ti = lax.broadcasted_iota(jnp.int32, (ln, ln), 0)
    si = lax.broadcasted_iota(jnp.int32, (ln, ln), 1)
    tb, sb = ti // c, si // c
    dist_blk = (sb - tb) if reverse else (tb - sb)
    causal = (si >= ti) if reverse else (si <= ti)
    a = jnp.where((dist_blk == 0) & causal, p_diag, 0.0)
    for dist in range(1, nb):
        a = jnp.where(dist_blk == dist, p_off[(dist - 1) * ln:dist * ln, :], a)

    o = _dot(a.astype(bf16), v.astype(bf16))
    q_in = qe * jnp.exp2(bstart)
    o = o + _dot_nt(q_in.astype(bf16), st.astype(bf16))
    khat = ke * bcast([jnp.exp2(btot - bend[i]) for i in range(nb)])
    st_new = st * jnp.exp2(btot) + _dot_tn(v.astype(bf16), khat.astype(bf16))
    return o, st_new


def _gla_kernel(q_ref, f_ref, v_ref, lb_ref, tri_ref, o_ref, st_ref, *, reverse):
    @pl.when(pl.program_id(2) == 0)
    def _():
        st_ref[...] = jnp.zeros_like(st_ref)

    rows, width = q_ref.shape
    hd = HEAD_DIM
    lb = lb_ref[...]
    f = lb + (1.0 - lb) * _sigmoid(f_ref[...].astype(f32))
    lf = jnp.log2(f)
    lf_hi, lf_lo = _split_bf16(lf)
    tri = tri_ref[...]
    b = _dot(tri, lf_hi) + _dot(tri, lf_lo)
    bx = b - lf
    kk = 1.0 - f
    qq = q_ref[...].astype(f32) * (float(hd) ** -0.5)
    vv = v_ref[...].astype(f32)

    n_chunks = rows // GLA_CHUNK
    order = range(n_chunks - 1, -1, -1) if reverse else range(n_chunks)
    for h in range(width // hd):
        cs = slice(h * hd, (h + 1) * hd)
        st = st_ref[h]
        for ci in order:
            rs = slice(ci * GLA_CHUNK, (ci + 1) * GLA_CHUNK)
            o, st = _gla_chunk(qq[rs, cs], kk[rs, cs], vv[rs, cs], b[rs, cs], bx[rs, cs], st, reverse)
            o_ref[rs, cs] = o
        st_ref[h] = st


def _gla_call(proj, lb_row, tri, f_section, seq, ctx_len, n_batch, reverse):
    d = proj.shape[1] // 5
    r = GLA_STEP_ROWS
    w = GLA_HEADS_PER_STEP * HEAD_DIM
    nl, nc = seq // r, ctx_len // r
    ctx_base = n_batch * nl
    wb = d // w

    def row_block(bi, s):
        cs = jnp.minimum(s, nc - 1)
        ls = jnp.maximum(s - nc, 0)
        if reverse:
            cs, ls = nc - 1 - cs, nl - 1 - ls
        return jnp.where(s < nc, ctx_base + bi * nc + cs, bi * nl + ls)

    def out_block(bi, s):
        ls = jnp.maximum(s - nc, 0)
        if reverse:
            ls = nl - 1 - ls
        return bi * nl + ls

    def sec(k):
        return pl.BlockSpec((r, w), lambda bi, hb, s: (row_block(bi, s), k * wb + hb))

    return pl.pallas_call(
        functools.partial(_gla_kernel, reverse=reverse),
        out_shape=jax.ShapeDtypeStruct((n_batch * seq, d), f32),
        grid=(n_batch, wb, nc + nl),
        in_specs=[
            sec(0), sec(f_section), sec(3),
            pl.BlockSpec((1, w), lambda bi, hb, s: (0, hb)),
            pl.BlockSpec((r, r), lambda bi, hb, s: (0, 0)),
        ],
        out_specs=pl.BlockSpec((r, w), lambda bi, hb, s: (out_block(bi, s), hb)),
        scratch_shapes=[pltpu.VMEM((GLA_HEADS_PER_STEP, HEAD_DIM, HEAD_DIM), f32)],
        compiler_params=_params("parallel", "parallel", "arbitrary"),
        name="gla_scan_rev" if reverse else "gla_scan_fwd",
    )(proj, proj, proj, lb_row, tri)


def _chunk_tri(rows, reverse):
    t = np.arange(rows)[:, None]
    s = np.arange(rows)[None, :]
    same = (t // GLA_CHUNK) == (s // GLA_CHUNK)
    tri = same & ((s >= t) if reverse else (s <= t))
    return jnp.asarray(tri, dtype=bf16)


def _readout_kernel(of_ref, ob_ref, g_ref, gn_ref, o_ref):
    hd = HEAD_DIM
    gn = gn_ref[...]
    for h in range(o_ref.shape[-1] // hd):
        cs = slice(h * hd, (h + 1) * hd)
        o = of_ref[:, cs] + ob_ref[:, cs]
        g = g_ref[:, cs].astype(f32)
        y = o * lax.rsqrt(jnp.mean(o * o, axis=-1, keepdims=True) + EPS) * gn
        o_ref[:, cs] = (y * (g * _sigmoid(g))).astype(o_ref.dtype)


def _readout_call(o_f, o_b, proj, gn_row, rows):
    d = o_f.shape[1]
    tm = _tile(rows, 256, 8)
    w = _tile(d, 512, HEAD_DIM)
    wb = d // w
    return pl.pallas_call(
        _readout_kernel,
        out_shape=jax.ShapeDtypeStruct((rows, d), bf16),
        grid=(rows // tm, wb),
        in_specs=[
            pl.BlockSpec((tm, w), lambda i, j: (i, j)),
            pl.BlockSpec((tm, w), lambda i, j: (i, j)),
            pl.BlockSpec((tm, w), lambda i, j: (i, 4 * wb + j)),
            pl.BlockSpec((1, HEAD_DIM), lambda i, j: (0, 0)),
        ],
        out_specs=pl.BlockSpec((tm, w), lambda i, j: (i, j)),
        compiler_params=_params("parallel", "arbitrary"),
        name="hgrn_readout",
    )(o_f, o_b, proj, gn_row)


def kernel(x, c, ctx, c_ctx, ada_w, ada_b, norm1_g, norm2_g, fnet_w_in, fnet_w_out, hgrn_w_in, hgrn_lb,
           hgrn_gnorm_g, hgrn_w_out, router_w, router_b, exp_w_gate, exp_w_up, exp_w_down, sh_w_gate,
           sh_w_up, sh_w_down, final_g):
    n_batch, seq, d = x.shape
    ctx_len = ctx.shape[1]
    depth = ada_w.shape[0]
    assert depth == 2 and n_batch + 1 <= MOD_ROWS
    assert seq % GLA_STEP_ROWS == 0 and ctx_len % GLA_STEP_ROWS == 0 and seq % ctx_len == 0
    n_lat, n_ctx = n_batch * seq, n_batch * ctx_len
    n_all = n_lat + n_ctx
    fdim = exp_w_gate.shape[-1]
    n_sh = sh_w_gate.shape[-1] // fdim

    lb_all = jnp.cumsum(jax.nn.softmax(hgrn_lb.astype(f32), axis=0), axis=0)
    lb_all = lb_all - lb_all[0]

    x_all = jnp.concatenate([x.reshape(n_lat, d), ctx.reshape(n_ctx, d)], axis=0)
    cvec = jnp.concatenate([c, c_ctx[None, :], jnp.zeros((MOD_ROWS - n_batch - 1, d), f32)], axis=0)
    ada_b3 = ada_b[:, None, :]

    def channel_mixer(i, x_in, mods, rows, last):
        rw_hi, rw_lo = _split_bf16(router_w[i].T)
        m, m_slabs, e_idx, rank, w_k, counts = _norm_mod_router_call(
            x_in, norm2_g[i][None, :], mods, (3, 4), rows, seq, n_batch, rw_hi, rw_lo, router_b[i][:, None])
        n_tiles, tile_expert, src, dest = _dispatch_plan(e_idx, rank, counts[:, 0], rows, MOE_TILE_ROWS)
        ys = _expert_call(m_slabs, exp_w_gate, exp_w_up, exp_w_down, i, tile_expert, src, n_tiles,
                          MOE_TILE_ROWS)
        split_in = lambda ws: ws.astype(bf16).reshape(d, n_sh, fdim).transpose(1, 0, 2)
        y_sh = _shared_call(m, split_in(sh_w_gate[i]), split_in(sh_w_up[i]),
                            sh_w_down[i].astype(bf16).reshape(n_sh, fdim, d), rows)
        return _combine_call(ys, dest, w_k.T, x_in, y_sh, mods, 5, rows, seq, n_batch,
                             final_g[None, :] if last else None)

    mods = _ada_call(cvec, ada_w, ada_b3, 0)
    a = _norm_mod_call(x_all, norm1_g[0][None, :], mods, (0, 1), n_all, seq, n_batch)
    u = _mm_call(a, fnet_w_in[0], bf16)
    cg = d // FNET_GROUPS
    cc, sc = _dft_tables(cg)
    pq = _chan_dft_call(u, jnp.concatenate([cc, sc], axis=1), n_all)
    z = jnp.concatenate([
        _pos_dft_call(pq, *_dft_tables(seq), seq, n_batch, 0),
        _pos_dft_call(pq, *_dft_tables(ctx_len), ctx_len, n_batch, n_lat // ctx_len),
    ], axis=0)
    x_all = _mm_resid_call(z, fnet_w_out[0], x_all, mods, 2, n_all, seq, n_batch)
    x_all = channel_mixer(0, x_all, mods, n_all, False)

    mods = _ada_call(cvec, ada_w, ada_b3, 1)
    a = _norm_mod_call(x_all, norm1_g[1][None, :], mods, (0, 1), n_all, seq, n_batch)
    proj = _mm_call(a, hgrn_w_in[0], bf16)
    o_f = _gla_call(proj, lb_all[1, 0][None, :], _chunk_tri(GLA_STEP_ROWS, False), 1, seq, ctx_len, n_batch, False)
    o_b = _gla_call(proj, lb_all[1, 1][None, :], _chunk_tri(GLA_STEP_ROWS, True), 2, seq, ctx_len, n_batch, True)
    r = _readout_call(o_f, o_b, proj, hgrn_gnorm_g[0][None, :], n_lat)
    x_lat = _mm_resid_call(r, hgrn_w_out[0], x_all, mods, 2, n_lat, seq, n_batch)
    out = channel_mixer(1, x_lat, mods, n_lat, True)
    return out.reshape(n_batch, seq, d)
```

```python
import functools
import math

import jax
import jax.numpy as jnp
import numpy as np
from jax import lax
from jax.experimental import pallas as pl
from jax.experimental.pallas import tpu as pltpu

EPS = 1e-6
FNET_GROUPS = 8
HEAD_DIM = 128
ROUTE_GROUPS = 8
ROUTE_TOPK_GROUPS = 4
ROUTE_TOP_K = 8
ROUTED_SCALE = 2.5
GLA_CHUNK = 64
GLA_BLOCK = 16
GLA_STEP_ROWS = 256
GLA_HEADS_PER_STEP = 8
MOD_ROWS = 8
MOE_TILE_ROWS = 256
LANES = 128
SLAB_PITCH = 40

V7X_VMEM_LIMIT_BYTES = 56 * 1024 * 1024

f32 = jnp.float32
bf16 = jnp.bfloat16


def _tile(n, pref, mult):
    if n <= pref:
        return n
    t = (pref // mult) * mult
    while t > 0 and n % t:
        t -= mult
    assert t > 0, (n, pref, mult)
    return t


def _params(*sem):
    return pltpu.CompilerParams(dimension_semantics=sem, vmem_limit_bytes=V7X_VMEM_LIMIT_BYTES)


def _sigmoid(x):
    return 1.0 / (1.0 + jnp.exp(-x))


def _dot(a, b):
    return jnp.dot(a, b, preferred_element_type=f32)


def _dot_nt(a, b):
    return lax.dot_general(a, b, (((1,), (1,)), ((), ())), preferred_element_type=f32)


def _dot_tn(a, b):
    return lax.dot_general(a, b, (((0,), (0,)), ((), ())), preferred_element_type=f32)


def _split_bf16(x):
    hi = x.astype(bf16)
    lo = (x - hi.astype(f32)).astype(bf16)
    return hi, lo


def _store_slabs(ref, rows_val, pitch):
    n = rows_val.shape[0]
    for s in range(rows_val.shape[1] // LANES):
        ref[pl.ds(s, n, stride=pitch), :] = rows_val[:, s * LANES:(s + 1) * LANES]


def _load_slab_chunk(ref, first_row, n, s, pitch):
    return ref[pl.ds(first_row + s, n, stride=pitch), :]


def _ada_kernel(c_ref, w_ref, b_ref, o_ref):
    c = c_ref[...]
    s_hi, s_lo = _split_bf16(c * _sigmoid(c))
    w = w_ref[...].astype(bf16)
    o_ref[...] = _dot(s_hi, w) + _dot(s_lo, w) + b_ref[...]


def _ada_call(cvec, ada_w, ada_b3, layer):
    rows, d = cvec.shape
    n6 = ada_w.shape[2]
    tn = _tile(n6, 512, 128)
    out = pl.pallas_call(
        _ada_kernel,
        out_shape=jax.ShapeDtypeStruct((rows, n6), f32),
        grid=(n6 // tn,),
        in_specs=[
            pl.BlockSpec((rows, d), lambda j: (0, 0)),
            pl.BlockSpec((None, d, tn), lambda j: (layer, 0, j)),
            pl.BlockSpec((None, 1, tn), lambda j: (layer, 0, j)),
        ],
        out_specs=pl.BlockSpec((rows, tn), lambda j: (0, j)),
        compiler_params=_params("arbitrary"),
        name="ada_mod",
    )(cvec, ada_w, ada_b3)
    return out.reshape(rows, 6, d).transpose(1, 0, 2)[:, :, None, :]


def _norm_mod(x, g, shift, scale):
    ms = jnp.mean(x * x, axis=-1, keepdims=True)
    y = x * lax.rsqrt(ms + EPS) * g
    return y * (1.0 + scale) + shift


def _norm_mod_kernel(x_ref, g_ref, sh_ref, sc_ref, o_ref):
    o_ref[...] = _norm_mod(x_ref[...], g_ref[...], sh_ref[...], sc_ref[...]).astype(o_ref.dtype)


def _group_of(i, tiles_per_batch, n_batch):
    return jnp.minimum(i // tiles_per_batch, n_batch)


def _norm_mod_call(x_all, g_row, mods, kinds, rows, seq, n_batch):
    d = x_all.shape[1]
    tm = _tile(math.gcd(seq, rows), 256, 8)
    tpb = seq // tm
    k_sh, k_sc = kinds
    return pl.pallas_call(
        _norm_mod_kernel,
        out_shape=jax.ShapeDtypeStruct((rows, d), bf16),
        grid=(rows // tm,),
        in_specs=[
            pl.BlockSpec((tm, d), lambda i: (i, 0)),
            pl.BlockSpec((1, d), lambda i: (0, 0)),
            pl.BlockSpec((None, None, 1, d), lambda i: (k_sh, _group_of(i, tpb, n_batch), 0, 0)),
            pl.BlockSpec((None, None, 1, d), lambda i: (k_sc, _group_of(i, tpb, n_batch), 0, 0)),
        ],
        out_specs=pl.BlockSpec((tm, d), lambda i: (i, 0)),
        compiler_params=_params("parallel"),
        name="norm_mod",
    )(x_all, g_row, mods, mods)


def _route(logits_t, bias_col, carry, before):
    n_exp, tm = logits_t.shape
    gs = n_exp // ROUTE_GROUPS
    neg = -jnp.inf
    scores = _sigmoid(logits_t)
    choice = scores + bias_col

    iota_g = lax.broadcasted_iota(jnp.int32, (gs, tm), 0)
    grp_rows = []
    for g in range(ROUTE_GROUPS):
        xg = choice[g * gs:(g + 1) * gs, :]
        m1 = jnp.max(xg, axis=0, keepdims=True)
        i1 = jnp.min(jnp.where(xg == m1, iota_g, gs), axis=0, keepdims=True)
        m2 = jnp.max(jnp.where(iota_g == i1, neg, xg), axis=0, keepdims=True)
        grp_rows.append(m1 + m2)
    gsc = jnp.concatenate(grp_rows, axis=0)

    iota_gr = lax.broadcasted_iota(jnp.int32, (ROUTE_GROUPS, tm), 0)
    gsel = jnp.zeros((ROUTE_GROUPS, tm), f32)
    for _ in range(ROUTE_TOPK_GROUPS):
        mx = jnp.max(gsc, axis=0, keepdims=True)
        ix = jnp.min(jnp.where(gsc == mx, iota_gr, ROUTE_GROUPS), axis=0, keepdims=True)
        hit = iota_gr == ix
        gsel = jnp.where(hit, 1.0, gsel)
        gsc = jnp.where(hit, neg, gsc)
    emask = jnp.concatenate(
        [jnp.broadcast_to(gsel[g:g + 1, :], (gs, tm)) for g in range(ROUTE_GROUPS)], axis=0)

    masked = jnp.where(emask > 0, choice, neg)
    iota_e = lax.broadcasted_iota(jnp.int32, (n_exp, tm), 0)
    sel = jnp.zeros((n_exp, tm), f32)
    hits, e_rows, w_rows = [], [], []
    for _ in range(ROUTE_TOP_K):
        mx = jnp.max(masked, axis=0, keepdims=True)
        ix = jnp.min(jnp.where(masked == mx, iota_e, n_exp), axis=0, keepdims=True)
        hit = iota_e == ix
        sel = jnp.where(hit, 1.0, sel)
        masked = jnp.where(hit, neg, masked)
        hits.append(hit)
        e_rows.append(ix)
        w_rows.append(jnp.sum(jnp.where(hit, scores, 0.0), axis=0, keepdims=True))
    pos = _dot(sel.astype(bf16), before) + carry
    p_rows = [jnp.sum(jnp.where(hit, pos, 0.0), axis=0, keepdims=True) for hit in hits]
    w = jnp.concatenate(w_rows, axis=0)
    w = w / jnp.sum(w, axis=0, keepdims=True) * ROUTED_SCALE
    new_carry = carry + jnp.sum(sel, axis=1, keepdims=True)
    return (jnp.concatenate(e_rows, axis=0), jnp.concatenate(p_rows, axis=0).astype(jnp.int32), w,
            new_carry)


def _norm_mod_router_kernel(x_ref, g_ref, sh_ref, sc_ref, rwh_ref, rwl_ref, rb_ref, before_ref,
                            m_ref, slab_ref, e_ref, p_ref, w_ref, cnt_ref, carry_ref):
    @pl.when(pl.program_id(0) == 0)
    def _():
        carry_ref[...] = jnp.zeros_like(carry_ref)

    y = _norm_mod(x_ref[...], g_ref[...], sh_ref[...], sc_ref[...])
    m_ref[...] = y.astype(m_ref.dtype)
    _store_slabs(slab_ref, y, y.shape[1] // LANES)
    y_hi, y_lo = _split_bf16(y)
    rwh = rwh_ref[...]
    logits_t = _dot_nt(rwh, y_hi) + _dot_nt(rwh, y_lo) + _dot_nt(rwl_ref[...], y_hi)
    e_idx, rank, w, carry = _route(logits_t, rb_ref[...], carry_ref[...], before_ref[...])
    e_ref[...] = e_idx
    p_ref[...] = rank
    w_ref[...] = w
    carry_ref[...] = carry
    cnt_ref[...] = carry.astype(jnp.int32)


def _norm_mod_router_call(x_all, g_row, mods, kinds, rows, seq, n_batch, rw_t_hi, rw_t_lo, rb_col):
    d = x_all.shape[1]
    n_sub = d // LANES
    n_exp = rw_t_hi.shape[0]
    tm = _tile(math.gcd(seq, rows), 256, 128)
    tpb = seq // tm
    k_sh, k_sc = kinds
    before = jnp.asarray(np.arange(tm)[:, None] < np.arange(tm)[None, :], dtype=bf16)
    top = pl.BlockSpec((ROUTE_TOP_K, tm), lambda i: (0, i))
    return pl.pallas_call(
        _norm_mod_router_kernel,
        out_shape=(jax.ShapeDtypeStruct((rows, d), bf16),
                   jax.ShapeDtypeStruct((rows * n_sub, LANES), f32),
                   jax.ShapeDtypeStruct((ROUTE_TOP_K, rows), jnp.int32),
                   jax.ShapeDtypeStruct((ROUTE_TOP_K, rows), jnp.int32),
                   jax.ShapeDtypeStruct((ROUTE_TOP_K, rows), f32),
                   jax.ShapeDtypeStruct((n_exp, 1), jnp.int32)),
        grid=(rows // tm,),
        in_specs=[
            pl.BlockSpec((tm, d), lambda i: (i, 0)),
            pl.BlockSpec((1, d), lambda i: (0, 0)),
            pl.BlockSpec((None, None, 1, d), lambda i: (k_sh, _group_of(i, tpb, n_batch), 0, 0)),
            pl.BlockSpec((None, None, 1, d), lambda i: (k_sc, _group_of(i, tpb, n_batch), 0, 0)),
            pl.BlockSpec((n_exp, d), lambda i: (0, 0)),
            pl.BlockSpec((n_exp, d), lambda i: (0, 0)),
            pl.BlockSpec((n_exp, 1), lambda i: (0, 0)),
            pl.BlockSpec((tm, tm), lambda i: (0, 0)),
        ],
        out_specs=(pl.BlockSpec((tm, d), lambda i: (i, 0)),
                   pl.BlockSpec((tm * n_sub, LANES), lambda i: (i, 0)), top, top, top,
                   pl.BlockSpec((n_exp, 1), lambda i: (0, 0))),
        scratch_shapes=[pltpu.VMEM((n_exp, 1), f32)],
        compiler_params=_params("arbitrary"),
        name="norm_mod_router",
    )(x_all, g_row, mods, mods, rw_t_hi, rw_t_lo, rb_col, before)


def _mm_kernel(a_ref, w_ref, o_ref):
    o_ref[...] = _dot(a_ref[...], w_ref[...]).astype(o_ref.dtype)


def _cast_weight_once(w_ref, wb_ref):
    @pl.when(pl.program_id(1) == 0)
    def _():
        wb_ref[...] = w_ref[...].astype(bf16)


def _mm_w32_kernel(a_ref, w_ref, o_ref, wb_ref):
    _cast_weight_once(w_ref, wb_ref)
    o_ref[...] = _dot(a_ref[...], wb_ref[...]).astype(o_ref.dtype)


def _mm_call(a, w, out_dtype, rows=None):
    rows = a.shape[0] if rows is None else rows
    k = a.shape[1]
    n = w.shape[1]
    tm = _tile(rows, 512, 8)
    tn = _tile(n, 1024, 128)
    return pl.pallas_call(
        _mm_w32_kernel,
        out_shape=jax.ShapeDtypeStruct((rows, n), out_dtype),
        grid=(n // tn, rows // tm),
        in_specs=[pl.BlockSpec((tm, k), lambda j, i: (i, 0)), pl.BlockSpec((k, tn), lambda j, i: (0, j))],
        out_specs=pl.BlockSpec((tm, tn), lambda j, i: (i, j)),
        scratch_shapes=[pltpu.VMEM((k, tn), bf16)],
        compiler_params=_params("arbitrary", "arbitrary"),
        name="matmul",
    )(a, w)


def _mm_resid_w32_kernel(a_ref, w_ref, x_ref, gate_ref, o_ref, wb_ref):
    _cast_weight_once(w_ref, wb_ref)
    o_ref[...] = x_ref[...] + gate_ref[...] * _dot(a_ref[...], wb_ref[...])


def _mm_resid_call(a, w, x_all, mods, kind, rows, seq, n_batch):
    k = a.shape[1]
    n = w.shape[1]
    tm = _tile(math.gcd(seq, rows), 512, 8)
    tn = _tile(n, 512, 128)
    tpb = seq // tm
    return pl.pallas_call(
        _mm_resid_w32_kernel,
        out_shape=jax.ShapeDtypeStruct((rows, n), f32),
        grid=(n // tn, rows // tm),
        in_specs=[
            pl.BlockSpec((tm, k), lambda j, i: (i, 0)),
            pl.BlockSpec((k, tn), lambda j, i: (0, j)),
            pl.BlockSpec((tm, tn), lambda j, i: (i, j)),
            pl.BlockSpec((None, None, 1, tn), lambda j, i: (kind, _group_of(i, tpb, n_batch), 0, j)),
        ],
        out_specs=pl.BlockSpec((tm, tn), lambda j, i: (i, j)),
        scratch_shapes=[pltpu.VMEM((k, tn), bf16)],
        compiler_params=_params("arbitrary", "arbitrary"),
        name="matmul_resid",
    )(a, w, x_all, mods)


def _dft_tables(n):
    r = 1 << (int(n).bit_length() // 2) if n & (n - 1) == 0 else 1
    col = jnp.arange(n, dtype=jnp.int32)[None, :]

    def rows_cs(row_ids):
        ang = ((row_ids[:, None] * col) % n).astype(f32) * (2.0 * np.pi / n)
        return jnp.cos(ang), jnp.sin(ang)

    ca, sa = rows_cs(jnp.arange(n // r, dtype=jnp.int32) * r)
    cb, sb = rows_cs(jnp.arange(r, dtype=jnp.int32))
    s = float(n) ** -0.5
    cos_t = (ca[:, None, :] * cb[None, :, :] - sa[:, None, :] * sb[None, :, :]).reshape(n, n)
    sin_t = (sa[:, None, :] * cb[None, :, :] + ca[:, None, :] * sb[None, :, :]).reshape(n, n)
    return (cos_t * s).astype(bf16), (sin_t * s).astype(bf16)


def _chan_dft_call(u, wcs, rows):
    d = u.shape[1]
    cg = d // FNET_GROUPS
    tm = _tile(rows, 512, 8)
    return pl.pallas_call(
        _mm_kernel,
        out_shape=jax.ShapeDtypeStruct((rows, 2 * d), bf16),
        grid=(rows // tm, FNET_GROUPS),
        in_specs=[pl.BlockSpec((tm, cg), lambda i, g: (i, g)), pl.BlockSpec((cg, 2 * cg), lambda i, g: (0, 0))],
        out_specs=pl.BlockSpec((tm, 2 * cg), lambda i, g: (i, g)),
        compiler_params=_params("parallel", "arbitrary"),
        name="fnet_chan_dft",
    )(u, wcs)


def _pos_dft_kernel(c_ref, s_ref, pq_ref, o_ref):
    cg = o_ref.shape[-1]
    acc = _dot(c_ref[...], pq_ref[:, :cg]) - _dot(s_ref[...], pq_ref[:, cg:])
    o_ref[...] = acc.astype(o_ref.dtype)


def _pos_dft_call(pq, cos_t, sin_t, seq, n_batch, first_block):
    d = pq.shape[1] // 2
    cg = d // FNET_GROUPS
    tm = _tile(seq, 512, 8)
    return pl.pallas_call(
        _pos_dft_kernel,
        out_shape=jax.ShapeDtypeStruct((n_batch * seq, d), bf16),
        grid=(n_batch, FNET_GROUPS, seq // tm),
        in_specs=[
            pl.BlockSpec((tm, seq), lambda b, g, i: (i, 0)),
            pl.BlockSpec((tm, seq), lambda b, g, i: (i, 0)),
            pl.BlockSpec((seq, 2 * cg), lambda b, g, i: (first_block + b, g)),
        ],
        out_specs=pl.BlockSpec((tm, cg), lambda b, g, i: (b * (seq // tm) + i, g)),
        compiler_params=_params("parallel", "parallel", "arbitrary"),
        name="fnet_pos_dft",
    )(cos_t, sin_t, pq)


def _glu(x, wg, wu, wd):
    hg = _dot(x, wg)
    hu = _dot(x, wu)
    return _dot(((hg * _sigmoid(hg)) * hu).astype(bf16), wd)


def _shared_kernel(m_ref, wg_ref, wu_ref, wd_ref, o_ref):
    j = pl.program_id(1)
    y = _glu(m_ref[...], wg_ref[...], wu_ref[...], wd_ref[...])

    @pl.when(j == 0)
    def _():
        o_ref[...] = y

    @pl.when(j > 0)
    def _():
        o_ref[...] += y


def _shared_call(m, wg, wu, wd, rows):
    d = m.shape[1]
    n_chunk, _, fdim = wg.shape
    tm = _tile(rows, 512, 8)
    return pl.pallas_call(
        _shared_kernel,
        out_shape=jax.ShapeDtypeStruct((rows, d), f32),
        grid=(rows // tm, n_chunk),
        in_specs=[
            pl.BlockSpec((tm, d), lambda i, j: (i, 0)),
            pl.BlockSpec((None, d, fdim), lambda i, j: (j, 0, 0)),
            pl.BlockSpec((None, d, fdim), lambda i, j: (j, 0, 0)),
            pl.BlockSpec((None, fdim, d), lambda i, j: (j, 0, 0)),
        ],
        out_specs=pl.BlockSpec((tm, d), lambda i, j: (i, 0)),
        compiler_params=_params("parallel", "arbitrary"),
        name="moe_shared",
    )(m, wg, wu, wd)


def _dispatch_plan(e_idx, rank, counts, rows, tile_rows):
    n_exp = counts.shape[0]
    n_tiles = (ROUTE_TOP_K * rows) // tile_rows + n_exp
    tiles_e = (counts + tile_rows - 1) // tile_rows
    tile_end = jnp.cumsum(tiles_e)
    first_slot = (tile_end - tiles_e) * tile_rows
    e_range = jnp.arange(n_exp, dtype=jnp.int32)
    dest = jnp.sum(jnp.where(e_idx[..., None] == e_range, first_slot, 0), axis=-1) + rank
    tile_ids = jnp.arange(n_tiles, dtype=jnp.int32)
    tile_expert = jnp.minimum(jnp.sum((tile_end[None, :] <= tile_ids[:, None]).astype(jnp.int32), axis=1),
                              n_exp - 1)
    tokens = jnp.broadcast_to(jnp.arange(rows, dtype=jnp.int32)[None, :], dest.shape)
    src = jnp.zeros(((n_tiles + 1) * tile_rows,), jnp.int32).at[dest.reshape(-1)].set(
        tokens.reshape(-1), unique_indices=True)
    return n_tiles, tile_expert, src.reshape(n_tiles + 1, 1, tile_rows), dest


def _slab_gather(idx_ref, n, n_sub, src_pitch, src_hbm, dst_buf, sem):
    for r in range(n):
        first = pl.multiple_of(idx_ref[0, r] * src_pitch, 8)
        pltpu.make_async_copy(src_hbm.at[pl.ds(first, n_sub), :],
                              dst_buf.at[pl.ds(r * SLAB_PITCH, n_sub), :], sem).start()


def _slab_gather_wait(n, n_sub, src_hbm, dst_buf, sem):
    pltpu.make_async_copy(src_hbm.at[pl.ds(0, n * n_sub), :], dst_buf.at[pl.ds(0, n * n_sub), :], sem).wait()


def _row_gather(idx_ref, n, src_hbm, dst_buf, sem):
    for r in range(n):
        pltpu.make_async_copy(src_hbm.at[pl.ds(idx_ref[0, r], 1), :], dst_buf.at[pl.ds(r, 1), :], sem).start()


def _row_gather_wait(n, src_hbm, dst_buf, sem):
    pltpu.make_async_copy(src_hbm.at[pl.ds(0, n), :], dst_buf, sem).wait()


def _expert_kernel(te_ref, src_cur_ref, src_nxt_ref, m_hbm, wg_ref, wu_ref, wd_ref, ys_ref,
                   xbuf, sem, wgb, wub, wdb):
    t = pl.program_id(0)
    n_t = pl.num_programs(0)
    tile_rows = ys_ref.shape[0]
    n_sub = wgb.shape[0] // LANES
    slot = t % 2

    @pl.when(t == 0)
    def _():
        _slab_gather(src_cur_ref, tile_rows, n_sub, n_sub, m_hbm, xbuf.at[0], sem.at[0])

    @pl.when((t == 0) | (te_ref[t] != te_ref[jnp.maximum(t - 1, 0)]))
    def _():
        wgb[...] = wg_ref[...].astype(bf16)
        wub[...] = wu_ref[...].astype(bf16)
        wdb[...] = wd_ref[...].astype(bf16)

    _slab_gather(src_nxt_ref, tile_rows, n_sub, n_sub, m_hbm, xbuf.at[1 - slot], sem.at[1 - slot])
    _slab_gather_wait(tile_rows, n_sub, m_hbm, xbuf.at[slot], sem.at[slot])
    x = jnp.concatenate(
        [_load_slab_chunk(xbuf.at[slot], 0, tile_rows, s, SLAB_PITCH).astype(bf16) for s in range(n_sub)],
        axis=1)
    ys_ref[...] = _glu(x, wgb[...], wub[...], wdb[...])

    @pl.when(t == n_t - 1)
    def _():
        _slab_gather_wait(tile_rows, n_sub, m_hbm, xbuf.at[1 - slot], sem.at[1 - slot])


def _expert_call(m, wg, wu, wd, layer, tile_expert, src, n_tiles, tile_rows):
    _, _, d, fdim = wg.shape
    smem_tile = lambda off: pl.BlockSpec((None, 1, tile_rows), lambda t, te: (t + off, 0, 0),
                                         memory_space=pltpu.SMEM)
    grid_spec = pltpu.PrefetchScalarGridSpec(
        num_scalar_prefetch=1,
        grid=(n_tiles,),
        in_specs=[
            smem_tile(0), smem_tile(1),
            pl.BlockSpec(memory_space=pl.ANY),
            pl.BlockSpec((None, None, d, fdim), lambda t, te: (layer, te[t], 0, 0)),
            pl.BlockSpec((None, None, d, fdim), lambda t, te: (layer, te[t], 0, 0)),
            pl.BlockSpec((None, None, fdim, d), lambda t, te: (layer, te[t], 0, 0)),
        ],
        out_specs=pl.BlockSpec((tile_rows, d), lambda t, te: (t, 0)),
        scratch_shapes=[
            pltpu.VMEM((2, tile_rows * SLAB_PITCH, LANES), f32),
            pltpu.SemaphoreType.DMA((2,)),
            pltpu.VMEM((d, fdim), bf16), pltpu.VMEM((d, fdim), bf16), pltpu.VMEM((fdim, d), bf16),
        ],
    )
    return pl.pallas_call(
        _expert_kernel,
        out_shape=jax.ShapeDtypeStruct((n_tiles * tile_rows, d), f32),
        grid_spec=grid_spec,
        compiler_params=_params("arbitrary"),
        name="moe_experts",
    )(tile_expert, src, src, m, wg, wu, wd)


def _combine_kernel(dst_cur_ref, dst_nxt_ref, ys_hbm, w_ref, x_ref, ysh_ref, gate_ref, *rest, final):
    if final:
        g_ref, o_ref, gbuf, sem = rest
    else:
        o_ref, gbuf, sem = rest
    i = pl.program_id(0)
    n_i = pl.num_programs(0)
    tm = o_ref.shape[0]
    n_rows = ROUTE_TOP_K * tm
    slot = i % 2

    @pl.when(i == 0)
    def _():
        _row_gather(dst_cur_ref, n_rows, ys_hbm, gbuf.at[0], sem.at[0])

    _row_gather(dst_nxt_ref, n_rows, ys_hbm, gbuf.at[1 - slot], sem.at[1 - slot])
    _row_gather_wait(n_rows, ys_hbm, gbuf.at[slot], sem.at[slot])

    w = w_ref[...]
    acc = ysh_ref[...]
    for k in range(ROUTE_TOP_K):
        acc = acc + w[:, k:k + 1] * gbuf[slot, k * tm:(k + 1) * tm, :]
    x = x_ref[...] + gate_ref[...] * acc
    if final:
        x = x * lax.rsqrt(jnp.mean(x * x, axis=-1, keepdims=True) + EPS) * g_ref[...]
    o_ref[...] = x

    @pl.when(i == n_i - 1)
    def _():
        _row_gather_wait(n_rows, ys_hbm, gbuf.at[1 - slot], sem.at[1 - slot])


def _combine_call(ys, dest, w_t, x_all, y_sh, mods, kind, rows, seq, n_batch, final_g_row=None):
    d = x_all.shape[1]
    tm = _tile(math.gcd(seq, rows), 64, 8)
    tpb = seq // tm
    n_i = rows // tm
    dst = dest.reshape(ROUTE_TOP_K, n_i, tm).transpose(1, 0, 2).reshape(n_i, 1, ROUTE_TOP_K * tm)
    dst = jnp.concatenate([dst, jnp.zeros((1, 1, ROUTE_TOP_K * tm), jnp.int32)], axis=0)
    smem_tile = lambda off: pl.BlockSpec((None, 1, ROUTE_TOP_K * tm), lambda i: (i + off, 0, 0),
                                         memory_space=pltpu.SMEM)
    in_specs = [
        smem_tile(0), smem_tile(1),
        pl.BlockSpec(memory_space=pl.ANY),
        pl.BlockSpec((tm, ROUTE_TOP_K), lambda i: (i, 0)),
        pl.BlockSpec((tm, d), lambda i: (i, 0)),
        pl.BlockSpec((tm, d), lambda i: (i, 0)),
        pl.BlockSpec((None, None, 1, d), lambda i: (kind, _group_of(i, tpb, n_batch), 0, 0)),
    ]
    args = [dst, dst, ys, w_t, x_all, y_sh, mods]
    if final_g_row is not None:
        in_specs.append(pl.BlockSpec((1, d), lambda i: (0, 0)))
        args.append(final_g_row)
    return pl.pallas_call(
        functools.partial(_combine_kernel, final=final_g_row is not None),
        out_shape=jax.ShapeDtypeStruct((rows, d), f32),
        grid=(n_i,),
        in_specs=in_specs,
        out_specs=pl.BlockSpec((tm, d), lambda i: (i, 0)),
        scratch_shapes=[pltpu.VMEM((2, ROUTE_TOP_K * tm, d), f32), pltpu.SemaphoreType.DMA((2,))],
        compiler_params=_params("arbitrary"),
        name="moe_combine",
    )(*args)


def _gla_chunk(q, k, v, b, bx, st, reverse):
    ln, hd = q.shape
    c = GLA_BLOCK
    nb = ln // c

    def row(a, r):
        return a[r:r + 1, :]

    def bcast(rows):
        return jnp.concatenate([jnp.broadcast_to(r, (c, hd)) for r in rows], axis=0)

    beta = [row(bx, c * i + (c - 1 if reverse else 0)) for i in range(nb)]
    bend = [row(b, c * i + (0 if reverse else c - 1)) for i in range(nb)]
    btot = row(b, 0 if reverse else ln - 1)
    pos = (lambda i: nb - 1 - i) if reverse else (lambda i: i)
    blk_at = pos

    bstart = bcast(beta)
    e_q = jnp.exp2(b - bstart)
    e_kd = jnp.exp2(bstart - b)
    e_ke = jnp.exp2(bcast(bend) - b)
    qe = q * e_q
    ke = k * e_ke

    lhs = [qe]
    for dist in range(2, nb):
        gap = []
        for i in range(nb):
            p = pos(i) - dist
            gap.append(jnp.exp2(beta[i] - bend[blk_at(p)]) if p >= 0 else jnp.ones((1, hd), f32))
        lhs.append(qe * bcast(gap))
    p_off = _dot_nt(jnp.concatenate(lhs, axis=0).astype(bf16), ke.astype(bf16))
    p_diag = _dot_nt(qe.astype(bf16), (k * e_kd).astype(bf16))

    ti = lax.broadcasted_iota(jnp.int32, (ln, ln), 0)
    si = lax.broadcasted_iota(jnp.int32, (ln, ln), 1)
    tb, sb = ti // c, si // c
    dist_blk = (sb - tb) if reverse else (tb - sb)
    causal = (si >= ti) if reverse else (si <= ti)
    a = jnp.where((dist_blk == 0) & causal, p_diag, 0.0)
    for dist in range(1, nb):
        a = jnp.where(dist_blk == dist, p_off[(dist - 1) * ln:dist * ln, :], a)

    o = _dot(a.astype(bf16), v.astype(bf16))
    q_in = qe * jnp.exp2(bstart)
    o = o + _dot_nt(q_in.astype(bf16), st.astype(bf16))
    khat = ke * bcast([jnp.exp2(btot - bend[i]) for i in range(nb)])
    st_new = st * jnp.exp2(btot) + _dot_tn(v.astype(bf16), khat.astype(bf16))
    return o, st_new


def _gla_kernel(q_ref, f_ref, v_ref, lb_ref, tri_ref, o_ref, st_ref, *, reverse):
    @pl.when(pl.program_id(2) == 0)
    def _():
        st_ref[...] = jnp.zeros_like(st_ref)

    rows, width = q_ref.shape
    hd = HEAD_DIM
    lb = lb_ref[...]
    f = lb + (1.0 - lb) * _sigmoid(f_ref[...].astype(f32))
    lf = jnp.log2(f)
    lf_hi, lf_lo = _split_bf16(lf)
    tri = tri_ref[...]
    b = _dot(tri, lf_hi) + _dot(tri, lf_lo)
    bx = b - lf
    kk = 1.0 - f
    qq = q_ref[...].astype(f32) * (float(hd) ** -0.5)
    vv = v_ref[...].astype(f32)

    n_chunks = rows // GLA_CHUNK
    order = range(n_chunks - 1, -1, -1) if reverse else range(n_chunks)
    for h in range(width // hd):
        cs = slice(h * hd, (h + 1) * hd)
        st = st_ref[h]
        for ci in order:
            rs = slice(ci * GLA_CHUNK, (ci + 1) * GLA_CHUNK)
            o, st = _gla_chunk(qq[rs, cs], kk[rs, cs], vv[rs, cs], b[rs, cs], bx[rs, cs], st, reverse)
            o_ref[rs, cs] = o
        st_ref[h] = st


def _gla_call(proj, lb_row, tri, f_section, seq, ctx_len, n_batch, reverse):
    d = proj.shape[1] // 5
    r = GLA_STEP_ROWS
    w = GLA_HEADS_PER_STEP * HEAD_DIM
    nl, nc = seq // r, ctx_len // r
    ctx_base = n_batch * nl
    wb = d // w

    def row_block(bi, s):
        cs = jnp.minimum(s, nc - 1)
        ls = jnp.maximum(s - nc, 0)
        if reverse:
            cs, ls = nc - 1 - cs, nl - 1 - ls
        return jnp.where(s < nc, ctx_base + bi * nc + cs, bi * nl + ls)

    def out_block(bi, s):
        ls = jnp.maximum(s - nc, 0)
        if reverse:
            ls = nl - 1 - ls
        return bi * nl + ls

    def sec(k):
        return pl.BlockSpec((r, w), lambda bi, hb, s: (row_block(bi, s), k * wb + hb))

    return pl.pallas_call(
        functools.partial(_gla_kernel, reverse=reverse),
        out_shape=jax.ShapeDtypeStruct((n_batch * seq, d), f32),
        grid=(n_batch, wb, nc + nl),
        in_specs=[
            sec(0), sec(f_section), sec(3),
            pl.BlockSpec((1, w), lambda bi, hb, s: (0, hb)),
            pl.BlockSpec((r, r), lambda bi, hb, s: (0, 0)),
        ],
        out_specs=pl.BlockSpec((r, w), lambda bi, hb, s: (out_block(bi, s), hb)),
        scratch_shapes=[pltpu.VMEM((GLA_HEADS_PER_STEP, HEAD_DIM, HEAD_DIM), f32)],
        compiler_params=_params("parallel", "parallel", "arbitrary"),
        name="gla_scan_rev" if reverse else "gla_scan_fwd",
    )(proj, proj, proj, lb_row, tri)


def _chunk_tri(rows, reverse):
    t = np.arange(rows)[:, None]
    s = np.arange(rows)[None, :]
    same = (t // GLA_CHUNK) == (s // GLA_CHUNK)
    tri = same & ((s >= t) if reverse else (s <= t))
    return jnp.asarray(tri, dtype=bf16)


def _readout_kernel(of_ref, ob_ref, g_ref, gn_ref, o_ref):
    hd = HEAD_DIM
    gn = gn_ref[...]
    for h in range(o_ref.shape[-1] // hd):
        cs = slice(h * hd, (h + 1) * hd)
        o = of_ref[:, cs] + ob_ref[:, cs]
        g = g_ref[:, cs].astype(f32)
        y = o * lax.rsqrt(jnp.mean(o * o, axis=-1, keepdims=True) + EPS) * gn
        o_ref[:, cs] = (y * (g * _sigmoid(g))).astype(o_ref.dtype)


def _readout_call(o_f, o_b, proj, gn_row, rows):
    d = o_f.shape[1]
    tm = _tile(rows, 256, 8)
    w = _tile(d, 512, HEAD_DIM)
    wb = d // w
    return pl.pallas_call(
        _readout_kernel,
        out_shape=jax.ShapeDtypeStruct((rows, d), bf16),
        grid=(rows // tm, wb),
        in_specs=[
            pl.BlockSpec((tm, w), lambda i, j: (i, j)),
            pl.BlockSpec((tm, w), lambda i, j: (i, j)),
            pl.BlockSpec((tm, w), lambda i, j: (i, 4 * wb + j)),
            pl.BlockSpec((1, HEAD_DIM), lambda i, j: (0, 0)),
        ],
        out_specs=pl.BlockSpec((tm, w), lambda i, j: (i, j)),
        compiler_params=_params("parallel", "arbitrary"),
        name="hgrn_readout",
    )(o_f, o_b, proj, gn_row)


def kernel(x, c, ctx, c_ctx, ada_w, ada_b, norm1_g, norm2_g, fnet_w_in, fnet_w_out, hgrn_w_in, hgrn_lb,
           hgrn_gnorm_g, hgrn_w_out, router_w, router_b, exp_w_gate, exp_w_up, exp_w_down, sh_w_gate,
           sh_w_up, sh_w_down, final_g):
    n_batch, seq, d = x.shape
    ctx_len = ctx.shape[1]
    depth = ada_w.shape[0]
    assert depth == 2 and n_batch + 1 <= MOD_ROWS
    assert seq % GLA_STEP_ROWS == 0 and ctx_len % GLA_STEP_ROWS == 0 and seq % ctx_len == 0
    n_lat, n_ctx = n_batch * seq, n_batch * ctx_len
    n_all = n_lat + n_ctx
    fdim = exp_w_gate.shape[-1]
    n_sh = sh_w_gate.shape[-1] // fdim

    lb_all = jnp.cumsum(jax.nn.softmax(hgrn_lb.astype(f32), axis=0), axis=0)
    lb_all = lb_all - lb_all[0]

    x_all = jnp.concatenate([x.reshape(n_lat, d), ctx.reshape(n_ctx, d)], axis=0)
    cvec = jnp.concatenate([c, c_ctx[None, :], jnp.zeros((MOD_ROWS - n_batch - 1, d), f32)], axis=0)
    ada_b3 = ada_b[:, None, :]

    def channel_mixer(i, x_in, mods, rows, last):
        rw_hi, rw_lo = _split_bf16(router_w[i].T)
        m, m_slabs, e_idx, rank, w_k, counts = _norm_mod_router_call(
            x_in, norm2_g[i][None, :], mods, (3, 4), rows, seq, n_batch, rw_hi, rw_lo, router_b[i][:, None])
        n_tiles, tile_expert, src, dest = _dispatch_plan(e_idx, rank, counts[:, 0], rows, MOE_TILE_ROWS)
        ys = _expert_call(m_slabs, exp_w_gate, exp_w_up, exp_w_down, i, tile_expert, src, n_tiles,
                          MOE_TILE_ROWS)
        split_in = lambda ws: ws.astype(bf16).reshape(d, n_sh, fdim).transpose(1, 0, 2)
        y_sh = _shared_call(m, split_in(sh_w_gate[i]), split_in(sh_w_up[i]),
                            sh_w_down[i].astype(bf16).reshape(n_sh, fdim, d), rows)
        return _combine_call(ys, dest, w_k.T, x_in, y_sh, mods, 5, rows, seq, n_batch,
                             final_g[None, :] if last else None)

    mods = _ada_call(cvec, ada_w, ada_b3, 0)
    a = _norm_mod_call(x_all, norm1_g[0][None, :], mods, (0, 1), n_all, seq, n_batch)
    u = _mm_call(a, fnet_w_in[0], bf16)
    cg = d // FNET_GROUPS
    cc, sc = _dft_tables(cg)
    pq = _chan_dft_call(u, jnp.concatenate([cc, sc], axis=1), n_all)
    z = jnp.concatenate([
        _pos_dft_call(pq, *_dft_tables(seq), seq, n_batch, 0),
        _pos_dft_call(pq, *_dft_tables(ctx_len), ctx_len, n_batch, n_lat // ctx_len),
    ], axis=0)
    x_all = _mm_resid_call(z, fnet_w_out[0], x_all, mods, 2, n_all, seq, n_batch)
    x_all = channel_mixer(0, x_all, mods, n_all, False)

    mods = _ada_call(cvec, ada_w, ada_b3, 1)
    a = _norm_mod_call(x_all, norm1_g[1][None, :], mods, (0, 1), n_all, seq, n_batch)
    proj = _mm_call(a, hgrn_w_in[0], bf16)
    o_f = _gla_call(proj, lb_all[1, 0][None, :], _chunk_tri(GLA_STEP_ROWS, False), 1, seq, ctx_len, n_batch, False)
    o_b = _gla_call(proj, lb_all[1, 1][None, :], _chunk_tri(GLA_STEP_ROWS, True), 2, seq, ctx_len, n_batch, True)
    r = _readout_call(o_f, o_b, proj, hgrn_gnorm_g[0][None, :], n_lat)
    x_lat = _mm_resid_call(r, hgrn_w_out[0], x_all, mods, 2, n_lat, seq, n_batch)
    out = channel_mixer(1, x_lat, mods, n_lat, True)
    return out.reshape(n_batch, seq, d)
```
